```python
import jax
import jax.numpy as jnp
from jax import lax
import numpy as np

D_MODEL = 2048
BATCH = 8
SEQ = 4096
DEPTH = 4

GRID_W = 64
CTX_LEN = 256
EPS = 1e-6
N_MOD = 6
N_BRANCH = 4

MLA_HEADS = 8
MLA_NOPE = 128
MLA_ROPE = 64
MLA_V = 128
Q_LORA = 512
KV_LORA = 512
ROPE_BASE = 10000.0
Q_BLOCK = 128

NA_HEADS = 8
NA_DIM = 128
WIN_R = 8
WIN_C = 16

LRU_W = 1024
LRU_BLOCKS = 8
LRU_C = 8.0
CONV_W = 4
CONV_PAD_L = 2

DN_HEADS = 8
DN_DK = 128
DN_DV = 128
DN_CHUNK = 64

N_EXPERTS = 32
TOP_K = 4
D_FF = 640
SWIGLU_ALPHA = 1.702
SWIGLU_LIMIT = 7.0

MLA_W = MLA_HEADS * MLA_V
NA_W = NA_HEADS * NA_DIM
DN_W = DN_HEADS * DN_DV
DN_QKV = DN_HEADS * (2 * DN_DK + DN_DV)
BRANCH_W = 1024
IN_SPLITS = (Q_LORA, KV_LORA, MLA_ROPE, NA_W, NA_W, NA_W, LRU_W, LRU_W, DN_QKV, DN_W, 2 * DN_HEADS, 2 * DN_HEADS, N_BRANCH * D_MODEL)
IN_COLS = Q_LORA + KV_LORA + MLA_ROPE + 3 * NA_W + 2 * LRU_W + DN_QKV + DN_W + 4 * DN_HEADS + N_BRANCH * D_MODEL

kernel_name = 'hybrid_diffusion_trunk'


def rms_norm(x, g):
    xf = x.astype(jnp.float32)
    y = xf * lax.rsqrt(jnp.mean(xf * xf, axis=-1, keepdims=True) + EPS)
    return (y * g.astype(jnp.float32)).astype(x.dtype)


def l2_normalize(x):
    xf = x.astype(jnp.float32)
    return (xf * lax.rsqrt(jnp.sum(xf * xf, axis=-1, keepdims=True) + EPS)).astype(x.dtype)


def modulate(h, shift, scale):
    return h * (1.0 + scale) + shift


def to_heads(t, n_heads):
    return t.reshape(t.shape[0], t.shape[1], n_heads, t.shape[2] // n_heads)


def flip_seq(ts, rev):
    return tuple(jnp.flip(t, axis=1) for t in ts) if rev else tuple(ts)


def split_columns(p):
    parts, start = [], 0
    for size in IN_SPLITS:
        parts.append(p[..., start:start + size])
        start += size
    return parts


def rope_2d_tables(n, dim):
    t = jnp.arange(n)
    row = (t // GRID_W).astype(jnp.float32)
    col = (t % GRID_W).astype(jnp.float32)
    nf = dim // 4
    inv_freq = ROPE_BASE ** (-jnp.arange(nf, dtype=jnp.float32) / nf)
    ang = jnp.stack([row[:, None] * inv_freq, col[:, None] * inv_freq], axis=1)
    return jnp.cos(ang), jnp.sin(ang)


def apply_rope_2d(x, cos, sin):
    B, n, H, dim = x.shape
    xr = x.reshape(B, n, H, 2, 2, dim // 4)
    a, b = xr[..., 0, :], xr[..., 1, :]
    c, s = cos[None, :, None], sin[None, :, None]
    out = jnp.stack([a * c - b * s, b * c + a * s], axis=-2)
    return out.reshape(x.shape).astype(x.dtype)


def short_conv(x, w, b=None):
    ch = x.shape[-1]
    y = lax.conv_general_dilated(x, w[:, None, :].astype(x.dtype), window_strides=(1,),
                                 padding=[(CONV_PAD_L, CONV_W - 1 - CONV_PAD_L)],
                                 dimension_numbers=('NWC', 'WIO', 'NWC'), feature_group_count=ch)
    if b is not None:
        y = y + b
    return y


def block_softmax_attention(q, k, v):
    B, S, H, dk = q.shape
    nb = S // Q_BLOCK
    scale = dk ** -0.5
    qb = jnp.moveaxis(q.reshape(B, nb, Q_BLOCK, H, dk), 1, 0)

    def one_block(qi):
        s = jnp.einsum('bqhd,bkhd->bhqk', qi, k).astype(jnp.float32) * scale
        p = jax.nn.softmax(s, axis=-1).astype(v.dtype)
        return jnp.einsum('bhqk,bkhd->bqhd', p, v)

    o = lax.map(one_block, qb)
    return jnp.moveaxis(o, 0, 1).reshape(B, S, H, v.shape[-1])


def mla_queries(q_c, qn_g, wq_up, rope):
    B, S, _ = q_c.shape
    q = (rms_norm(q_c, qn_g) @ wq_up).reshape(B, S, MLA_HEADS, MLA_NOPE + MLA_ROPE)
    q_nope, q_pe = q[..., :MLA_NOPE], q[..., MLA_NOPE:]
    if rope is not None:
        q_pe = apply_rope_2d(q_pe, *rope)
    return jnp.concatenate([q_nope, q_pe], axis=-1)


def mla_keys_values(kv_c, k_pe, kvn_g, wkv_up, rope):
    B, S, _ = kv_c.shape
    kv = (rms_norm(kv_c, kvn_g) @ wkv_up).reshape(B, S, MLA_HEADS, MLA_NOPE + MLA_V)
    k_nope, v = kv[..., :MLA_NOPE], kv[..., MLA_NOPE:]
    k_pe = k_pe[:, :, None, :]
    if rope is not None:
        k_pe = apply_rope_2d(k_pe, *rope)
    k_pe = jnp.broadcast_to(k_pe, (B, S, MLA_HEADS, MLA_ROPE))
    return jnp.concatenate([k_nope, k_pe], axis=-1), v


def neighbourhood_attention(q, k, v, k_ctx, v_ctx, rpb):
    B, N, H, d = q.shape
    rows = N // GRID_W
    wr = min(WIN_R, rows)
    n_loc = wr * WIN_C
    scale = d ** -0.5
    col = jnp.arange(GRID_W)
    col_start = jnp.clip(col - WIN_C // 2, 0, GRID_W - WIN_C)
    col_idx = col_start[:, None] + jnp.arange(WIN_C)
    col_off = col_idx - col[:, None] + (WIN_C - 1)
    qg = q.reshape(B, rows, GRID_W, H, d)
    kg = k.reshape(B, rows, GRID_W, H, d)
    vg = v.reshape(B, rows, GRID_W, H, d)

    def one_row(r):
        row_start = jnp.clip(r - wr // 2, 0, rows - wr)
        k_band = lax.dynamic_slice_in_dim(kg, row_start, wr, axis=1)
        v_band = lax.dynamic_slice_in_dim(vg, row_start, wr, axis=1)
        k_nb = k_band[:, :, col_idx]
        v_nb = v_band[:, :, col_idx]
        q_row = lax.dynamic_index_in_dim(qg, r, axis=1, keepdims=False)
        row_off = row_start + jnp.arange(wr) - r + (WIN_R - 1)
        bias = rpb[:, row_off[:, None, None], col_off[None, :, :]]
        s_loc = jnp.einsum('bqhd,brqjhd->bhqrj', q_row, k_nb).astype(jnp.float32) * scale
        s_loc = s_loc + jnp.transpose(bias, (0, 2, 1, 3)).astype(jnp.float32)[None]
        s_ctx = jnp.einsum('bqhd,bkhd->bhqk', q_row, k_ctx).astype(jnp.float32) * scale
        s = jnp.concatenate([s_loc.reshape(B, H, GRID_W, n_loc), s_ctx], axis=-1)
        p = jax.nn.softmax(s, axis=-1).astype(v.dtype)
        p_loc = p[..., :n_loc].reshape(B, H, GRID_W, wr, WIN_C)
        return (jnp.einsum('bhqrj,brqjhd->bqhd', p_loc, v_nb)
                + jnp.einsum('bhqk,bkhd->bqhd', p[..., n_loc:], v_ctx))

    o = lax.map(one_row, jnp.arange(rows))
    return jnp.moveaxis(o, 0, 1).reshape(B, N, H, d)


def rglru_coeffs(u, wa, ba, wx, bx, lam):
    B, S, W = u.shape
    ub = u.reshape(B, S, LRU_BLOCKS, W // LRU_BLOCKS)
    r = jax.nn.sigmoid(jnp.einsum('bsnj,njk->bsnk', ub, wa).reshape(B, S, W) + ba)
    i = jax.nn.sigmoid(jnp.einsum('bsnj,njk->bsnk', ub, wx).reshape(B, S, W) + bx)
    log_a = (-LRU_C * r.astype(jnp.float32)) * jax.nn.softplus(-lam.astype(jnp.float32))
    a = jnp.exp(log_a)
    b = jnp.sqrt(-jnp.expm1(2.0 * log_a)) * (i * u).astype(jnp.float32)
    return a, b


def linear_scan(a, b, h0, reverse):
    def combine(left, right):
        a_l, b_l = left
        a_r, b_r = right
        return a_l * a_r, a_r * b_l + b_r

    a_cum, h = lax.associative_scan(combine, (a, b), axis=1, reverse=reverse)
    return h + a_cum * h0[:, None]


def deltanet_qkv(qkv, conv_w):
    u = jax.nn.silu(short_conv(qkv, conv_w))
    nq = DN_HEADS * DN_DK
    q = l2_normalize(to_heads(u[..., :nq], DN_HEADS)) * (DN_DK ** -0.5)
    k = l2_normalize(to_heads(u[..., nq:2 * nq], DN_HEADS))
    v = to_heads(u[..., 2 * nq:], DN_HEADS)
    return q, k, v


def deltanet_gates(a_raw, b_raw, a_log, dt_bias):
    g = -jnp.exp(a_log.astype(jnp.float32)) * jax.nn.softplus(a_raw.astype(jnp.float32) + dt_bias)
    beta = jax.nn.sigmoid(b_raw.astype(jnp.float32))
    return g, beta


def gated_delta_chunked(q, k, v, g, beta, s0):
    B, S, H, dk = q.shape
    dv = v.shape[-1]
    out_dtype = v.dtype
    C = DN_CHUNK
    n = S // C
    f32 = jnp.float32

    def chunks(t):
        t = t.astype(f32).reshape(B, n, C, H, *t.shape[3:])
        return jnp.moveaxis(t, 3, 1)

    q, k, v, beta = chunks(q), chunks(k), chunks(v), chunks(beta)
    gc = jnp.cumsum(chunks(g), axis=-1)
    causal = jnp.tril(jnp.ones((C, C), bool))
    strict = jnp.tril(jnp.ones((C, C), bool), -1)
    diff = gc[..., :, None] - gc[..., None, :]
    decay = jnp.where(causal, jnp.exp(jnp.where(causal, diff, 0.0)), 0.0)
    kb = k * beta[..., None]
    lower = jnp.where(strict, jnp.einsum('bhnid,bhnjd->bhnij', kb, k) * decay, 0.0)
    eye = jnp.eye(C, dtype=f32)
    t_inv = lax.linalg.triangular_solve(lower + eye, jnp.broadcast_to(eye, lower.shape),
                                        left_side=True, lower=True, unit_diagonal=True)
    w_v = t_inv @ (v * beta[..., None])
    w_k = t_inv @ (kb * jnp.exp(gc)[..., None])
    a_intra = jnp.einsum('bhnid,bhnjd->bhnij', q, k) * decay
    q_dec = q * jnp.exp(gc)[..., None]
    k_dec = k * jnp.exp(gc[..., -1:] - gc)[..., None]
    g_end = jnp.exp(gc[..., -1])

    def step(state, xs):
        wv, wk, qd, kd, a, ge = xs
        v_new = wv - jnp.einsum('bhcd,bhde->bhce', wk, state)
        o = jnp.einsum('bhcd,bhde->bhce', qd, state) + jnp.einsum('bhcj,bhje->bhce', a, v_new)
        state = state * ge[..., None, None] + jnp.einsum('bhcd,bhce->bhde', kd, v_new)
        return state, o

    xs = tuple(jnp.moveaxis(t, 2, 0) for t in (w_v, w_k, q_dec, k_dec, a_intra, g_end))
    s_fin, o = lax.scan(step, s0.astype(f32), xs)
    o = jnp.moveaxis(jnp.moveaxis(o, 0, 2), 1, 3).reshape(B, S, H, dv)
    return o.astype(out_dtype), s_fin


def deltanet_output(o, z, norm_g):
    return (rms_norm(o, norm_g) * jax.nn.silu(to_heads(z, DN_HEADS))).reshape(z.shape)


def merge_branches(branches, gate_logits, w_branch, w_out):
    g = gate_logits.reshape(*gate_logits.shape[:-1], N_BRANCH, D_MODEL)
    merged = jnp.zeros_like(gate_logits[..., :D_MODEL])
    for i, o in enumerate(branches):
        merged = merged + jax.nn.sigmoid(g[..., i, :]) * (o @ w_branch[i])
    return merged @ w_out


def clamped_swiglu(u):
    glu, lin = u[..., ::2], u[..., 1::2]
    glu = jnp.minimum(glu, SWIGLU_LIMIT)
    lin = jnp.clip(lin, -SWIGLU_LIMIT, SWIGLU_LIMIT)
    return glu * jax.nn.sigmoid(SWIGLU_ALPHA * glu) * (lin + 1.0)


def moe_ffn(t, router_w, router_b, w1, b1, w2, b2):
    logits = (t @ router_w + router_b).astype(jnp.float32)
    top_val, top_idx = lax.top_k(logits, TOP_K)
    top_w = jax.nn.softmax(top_val, axis=-1)
    combine = jnp.einsum('tk,tke->te', top_w,
                         jax.nn.one_hot(top_idx, N_EXPERTS, dtype=jnp.float32)).astype(t.dtype)
    out = jnp.zeros_like(t)
    for e in range(N_EXPERTS):
        y = clamped_swiglu(t @ w1[e] + b1[e]) @ w2[e] + b2[e]
        out = out + combine[:, e:e + 1] * y
    return out


def token_mixers(hx, hz, rope, need_ctx, w_in, mla_qn_g, mla_wq_up, mla_kvn_g, mla_wkv_up, na_rpb,
                 lru_conv_w, lru_conv_b, lru_wa, lru_ba, lru_wx, lru_bx, lru_lam,
                 dn_conv_w, dn_a_log, dn_dt_bias, dn_norm_g, w_branch, w_out):
    B, N, _ = hx.shape
    M = hz.shape[1]
    (qc_x, kvc_x, kpe_x, naq_x, nak_x, nav_x, lu_x, ly_x, dqkv_x, dz_x, da_x, db_x, gt_x) = split_columns(hx @ w_in)
    (qc_z, kvc_z, kpe_z, naq_z, nak_z, nav_z, lu_z, ly_z, dqkv_z, dz_z, da_z, db_z, gt_z) = split_columns(hz @ w_in)

    k_x, v_x = mla_keys_values(kvc_x, kpe_x, mla_kvn_g, mla_wkv_up, rope)
    k_z, v_z = mla_keys_values(kvc_z, kpe_z, mla_kvn_g, mla_wkv_up, None)
    q_x = mla_queries(qc_x, mla_qn_g, mla_wq_up, rope)
    o_a_x = block_softmax_attention(q_x, jnp.concatenate([k_x, k_z], axis=1),
                                    jnp.concatenate([v_x, v_z], axis=1)).reshape(B, N, MLA_W)

    nk_z, nv_z = to_heads(nak_z, NA_HEADS), to_heads(nav_z, NA_HEADS)
    o_b_x = neighbourhood_attention(to_heads(naq_x, NA_HEADS), to_heads(nak_x, NA_HEADS),
                                    to_heads(nav_x, NA_HEADS), nk_z, nv_z, na_rpb).reshape(B, N, NA_W)

    u_x = short_conv(lu_x, lru_conv_w, lru_conv_b)
    u_z = short_conv(lu_z, lru_conv_w, lru_conv_b)
    hs_x, hs_z = [], []
    for d, rev in enumerate((False, True)):
        a_z, b_z = rglru_coeffs(u_z, lru_wa[d], lru_ba[d], lru_wx[d], lru_bx[d], lru_lam[d])
        h_z = linear_scan(a_z, b_z, jnp.zeros_like(b_z[:, 0]), rev)
        h_end = h_z[:, 0] if rev else h_z[:, -1]
        a_x, b_x = rglru_coeffs(u_x, lru_wa[d], lru_ba[d], lru_wx[d], lru_bx[d], lru_lam[d])
        hs_x.append(linear_scan(a_x, b_x, h_end, rev))
        hs_z.append(h_z)
    o_c_x = (hs_x[0] + hs_x[1]).astype(hx.dtype) * jax.nn.gelu(ly_x)

    qd_x, kd_x, vd_x = deltanet_qkv(dqkv_x, dn_conv_w)
    qd_z, kd_z, vd_z = deltanet_qkv(dqkv_z, dn_conv_w)
    da_x, db_x = da_x.reshape(B, N, 2, DN_HEADS), db_x.reshape(B, N, 2, DN_HEADS)
    da_z, db_z = da_z.reshape(B, M, 2, DN_HEADS), db_z.reshape(B, M, 2, DN_HEADS)
    od_x, od_z = [], []
    for d, rev in enumerate((False, True)):
        g_z, beta_z = deltanet_gates(da_z[:, :, d], db_z[:, :, d], dn_a_log[d], dn_dt_bias[d])
        g_x, beta_x = deltanet_gates(da_x[:, :, d], db_x[:, :, d], dn_a_log[d], dn_dt_bias[d])
        s_init = jnp.zeros((B, DN_HEADS, DN_DK, DN_DV), jnp.float32)
        o_z, s_z = gated_delta_chunked(*flip_seq((qd_z, kd_z, vd_z, g_z, beta_z), rev), s_init)
        o_x, _ = gated_delta_chunked(*flip_seq((qd_x, kd_x, vd_x, g_x, beta_x), rev), s_z)
        od_x.append(flip_seq((o_x,), rev)[0])
        od_z.append(flip_seq((o_z,), rev)[0])
    o_d_x = deltanet_output(od_x[0] + od_x[1], dz_x, dn_norm_g)

    out_x = merge_branches((o_a_x, o_b_x, o_c_x, o_d_x), gt_x, w_branch, w_out)
    if not need_ctx:
        return out_x, None

    o_a_z = block_softmax_attention(mla_queries(qc_z, mla_qn_g, mla_wq_up, None), k_z, v_z).reshape(B, M, MLA_W)
    o_b_z = block_softmax_attention(to_heads(naq_z, NA_HEADS), nk_z, nv_z).reshape(B, M, NA_W)
    o_c_z = (hs_z[0] + hs_z[1]).astype(hz.dtype) * jax.nn.gelu(ly_z)
    o_d_z = deltanet_output(od_z[0] + od_z[1], dz_z, dn_norm_g)
    out_z = merge_branches((o_a_z, o_b_z, o_c_z, o_d_z), gt_z, w_branch, w_out)
    return out_x, out_z


def setup_inputs(seed: int = 0) -> dict:
    key = jax.random.key(seed)
    ks = iter(jax.random.split(key, 48))
    f32 = jnp.float32
    L, D, E = DEPTH, D_MODEL, N_EXPERTS

    def nrm(shape, scale):
        return jax.random.normal(next(ks), shape, f32) * scale

    def gain(shape):
        return 1.0 + nrm(shape, 0.05)

    a_c = jax.random.uniform(next(ks), (L, 2, LRU_W), f32, 0.9, 0.999)
    a_base = a_c ** (1.0 / LRU_C)
    lru_lam = jnp.log(a_base) - jnp.log1p(-a_base)
    dn_a_log = jnp.log(jax.random.uniform(next(ks), (L, 2, DN_HEADS), f32, 1.0, 16.0))
    return {
        'x': nrm((BATCH, SEQ, D), 1.0),
        'c': nrm((BATCH, D), 1.0),
        'ctx': nrm((BATCH, CTX_LEN, D), 1.0),
        'c_ctx': nrm((D,), 1.0),
        'ada_w': nrm((L, D, N_MOD * D), 0.5 * D ** -0.5),
        'ada_b': nrm((L, N_MOD * D), 0.01),
        'norm_mix_g': gain((L, D)),
        'norm_ffn_g': gain((L, D)),
        'w_in': nrm((L, D, IN_COLS), D ** -0.5),
        'mla_qn_g': gain((L, Q_LORA)),
        'mla_wq_up': nrm((L, Q_LORA, MLA_HEADS * (MLA_NOPE + MLA_ROPE)), Q_LORA ** -0.5),
        'mla_kvn_g': gain((L, KV_LORA)),
        'mla_wkv_up': nrm((L, KV_LORA, MLA_HEADS * (MLA_NOPE + MLA_V)), KV_LORA ** -0.5),
        'na_rpb': nrm((L, NA_HEADS, 2 * WIN_R - 1, 2 * WIN_C - 1), 0.1),
        'lru_conv_w': nrm((L, CONV_W, LRU_W), CONV_W ** -0.5),
        'lru_conv_b': nrm((L, LRU_W), 0.01),
        'lru_wa': nrm((L, 2, LRU_BLOCKS, LRU_W // LRU_BLOCKS, LRU_W // LRU_BLOCKS), (LRU_W // LRU_BLOCKS) ** -0.5),
        'lru_ba': nrm((L, 2, LRU_W), 0.1),
        'lru_wx': nrm((L, 2, LRU_BLOCKS, LRU_W // LRU_BLOCKS, LRU_W // LRU_BLOCKS), (LRU_W // LRU_BLOCKS) ** -0.5),
        'lru_bx': nrm((L, 2, LRU_W), 0.1),
        'lru_lam': lru_lam,
        'dn_conv_w': nrm((L, CONV_W, DN_QKV), CONV_W ** -0.5),
        'dn_a_log': dn_a_log,
        'dn_dt_bias': nrm((L, 2, DN_HEADS), 0.1),
        'dn_norm_g': gain((L, DN_DV)),
        'w_branch': nrm((L, N_BRANCH, BRANCH_W, D), BRANCH_W ** -0.5),
        'w_out': nrm((L, D, D), D ** -0.5),
        'router_w': nrm((L, D, E), D ** -0.5),
        'router_b': nrm((L, E), 0.01),
        'exp_w1': nrm((L, E, D, 2 * D_FF), D ** -0.5),
        'exp_b1': nrm((L, E, 2 * D_FF), 0.01),
        'exp_w2': nrm((L, E, D_FF, D), D_FF ** -0.5),
        'exp_b2': nrm((L, E, D), 0.01),
        'final_g': gain((D,)),
    }


def reference(x, c, ctx, c_ctx, ada_w, ada_b, norm_mix_g, norm_ffn_g, w_in, mla_qn_g, mla_wq_up,
              mla_kvn_g, mla_wkv_up, na_rpb, lru_conv_w, lru_conv_b, lru_wa, lru_ba, lru_wx, lru_bx,
              lru_lam, dn_conv_w, dn_a_log, dn_dt_bias, dn_norm_g, w_branch, w_out, router_w, router_b,
              exp_w1, exp_b1, exp_w2, exp_b2, final_g):
    B, N, D = x.shape
    M = ctx.shape[1]
    rope = rope_2d_tables(N, MLA_ROPE)
    silu_c = jax.nn.silu(c)
    silu_cc = jax.nn.silu(c_ctx)
    z = ctx
    for l in range(DEPTH):
        last = l == DEPTH - 1
        mod_x = (silu_c @ ada_w[l] + ada_b[l]).reshape(B, N_MOD, 1, D)
        mod_z = (silu_cc @ ada_w[l] + ada_b[l]).reshape(N_MOD, D)
        hx = modulate(rms_norm(x, norm_mix_g[l]), mod_x[:, 0], mod_x[:, 1])
        hz = modulate(rms_norm(z, norm_mix_g[l]), mod_z[0], mod_z[1])
        mix_x, mix_z = token_mixers(hx, hz, rope, not last, w_in[l], mla_qn_g[l], mla_wq_up[l],
                                    mla_kvn_g[l], mla_wkv_up[l], na_rpb[l], lru_conv_w[l], lru_conv_b[l],
                                    lru_wa[l], lru_ba[l], lru_wx[l], lru_bx[l], lru_lam[l], dn_conv_w[l],
                                    dn_a_log[l], dn_dt_bias[l], dn_norm_g[l], w_branch[l], w_out[l])
        x = x + mod_x[:, 2] * mix_x
        hx = modulate(rms_norm(x, norm_ffn_g[l]), mod_x[:, 3], mod_x[:, 4])
        if last:
            ffn = moe_ffn(hx.reshape(B * N, D), router_w[l], router_b[l],
                          exp_w1[l], exp_b1[l], exp_w2[l], exp_b2[l])
            x = x + mod_x[:, 5] * ffn.reshape(B, N, D)
        else:
            z = z + mod_z[2] * mix_z
            hz = modulate(rms_norm(z, norm_ffn_g[l]), mod_z[3], mod_z[4])
            tokens = jnp.concatenate([hx.reshape(B * N, D), hz.reshape(B * M, D)], axis=0)
            ffn = moe_ffn(tokens, router_w[l], router_b[l], exp_w1[l], exp_b1[l], exp_w2[l], exp_b2[l])
            x = x + mod_x[:, 5] * ffn[:B * N].reshape(B, N, D)
            z = z + mod_z[5] * ffn[B * N:].reshape(B, M, D)
    return rms_norm(x, final_g)
```

```python
import functools
import math

import jax
import jax.numpy as jnp
from jax import lax
from jax.experimental import pallas as pl
from jax.experimental.pallas import tpu as pltpu

GRID_W = 64
EPS = 1e-6
N_MOD = 6
N_BRANCH = 4

MLA_HEADS = 8
MLA_NOPE = 128
MLA_ROPE = 64
MLA_V = 128
Q_LORA = 512
KV_LORA = 512
ROPE_BASE = 10000.0

NA_HEADS = 8
NA_DIM = 128
WIN_R = 8
WIN_C = 16

LRU_W = 1024
LRU_BLOCKS = 8
LRU_C = 8.0
CONV_W = 4
CONV_PAD_L = 2

DN_HEADS = 8
DN_DK = 128
DN_DV = 128
DN_CHUNK = 64

N_EXPERTS = 32
TOP_K = 4
D_FF = 640
SWIGLU_ALPHA = 1.702
SWIGLU_LIMIT = 7.0

ROW_TILE = 256
LANES = 128
NEG_BIG = -1e30
VMEM_LIMIT = 56 * 1024 * 1024

f32 = jnp.float32
bf16 = jnp.bfloat16


def _params(sem, vmem=VMEM_LIMIT):
    return pltpu.CompilerParams(dimension_semantics=sem, vmem_limit_bytes=vmem)


def _sigmoid(x):
    return 1.0 / (1.0 + jnp.exp(-x))


def _silu(x):
    return x * _sigmoid(x)


def _softplus(x):
    return jnp.maximum(x, 0.0) + jnp.log1p(jnp.exp(-jnp.abs(x)))


def _gelu_tanh(x):
    return 0.5 * x * (1.0 + jnp.tanh(math.sqrt(2.0 / math.pi) * (x + 0.044715 * x * x * x)))


def _mm_kernel(x_ref, w_ref, o_ref):
    o_ref[...] = jnp.dot(x_ref[...], w_ref[...], preferred_element_type=f32).astype(o_ref.dtype)


def matmul(x, w, out_dtype, tm, tn, name):
    m, k = x.shape
    n = w.shape[1]
    while m % tm:
        tm //= 2
    tn = min(tn, n)
    assert tm % 8 == 0 and n % tn == 0, (m, n, tm, tn)
    return pl.pallas_call(
        _mm_kernel,
        out_shape=jax.ShapeDtypeStruct((m, n), out_dtype),
        grid=(n // tn, m // tm),
        in_specs=[pl.BlockSpec((tm, k), lambda j, i: (i, 0)),
                  pl.BlockSpec((k, tn), lambda j, i: (0, j))],
        out_specs=pl.BlockSpec((tm, tn), lambda j, i: (i, j)),
        compiler_params=_params(("parallel", "parallel")),
        name=name,
    )(x, w)


def _ada_kernel(c_ref, w_ref, b_ref, o_ref):
    a = _silu(c_ref[...])
    o_ref[...] = jnp.dot(a, w_ref[...], preferred_element_type=f32,
                         precision=lax.Precision.HIGHEST) + b_ref[...]


def ada_modulation(cond, ada_w, ada_b):
    depth, d, n = ada_w.shape
    r = cond.shape[0]
    tn = 512
    return pl.pallas_call(
        _ada_kernel,
        out_shape=jax.ShapeDtypeStruct((depth, r, n), f32),
        grid=(depth, n // tn),
        in_specs=[pl.BlockSpec((r, d), lambda l, j: (0, 0)),
                  pl.BlockSpec((None, d, tn), lambda l, j: (l, 0, j)),
                  pl.BlockSpec((None, 1, tn), lambda l, j: (l, 0, j))],
        out_specs=pl.BlockSpec((None, r, tn), lambda l, j: (l, 0, j)),
        compiler_params=_params(("parallel", "parallel")),
        name="ada_modulation",
    )(cond, ada_w, ada_b.reshape(depth, 1, n))


def _mod_row(i, tiles_per_batch, lat_tiles, n_batch):
    return jnp.where(i % tiles_per_batch < lat_tiles, i // tiles_per_batch, n_batch)


def _norm_mod_kernel(x_ref, g_ref, mod_ref, *o_refs, shift_row):
    x = x_ref[...]
    y = x * lax.rsqrt(jnp.mean(x * x, axis=-1, keepdims=True) + EPS) * g_ref[...]
    h = y * (1.0 + mod_ref[shift_row + 1:shift_row + 2, :]) + mod_ref[shift_row:shift_row + 1, :]
    for o_ref in o_refs:
        o_ref[...] = h.astype(o_ref.dtype)


def norm_modulate(x, g, mod, shift_row, geom, out_dtypes):
    t, d = x.shape
    n_batch, tpb, lat = geom
    mrow = functools.partial(_mod_row, tiles_per_batch=tpb, lat_tiles=lat, n_batch=n_batch)
    outs = pl.pallas_call(
        functools.partial(_norm_mod_kernel, shift_row=shift_row),
        out_shape=[jax.ShapeDtypeStruct((t, d), dt) for dt in out_dtypes],
        grid=(t // ROW_TILE,),
        in_specs=[pl.BlockSpec((ROW_TILE, d), lambda i: (i, 0)),
                  pl.BlockSpec((1, d), lambda i: (0, 0)),
                  pl.BlockSpec((None, N_MOD, d), lambda i: (mrow(i), 0, 0))],
        out_specs=[pl.BlockSpec((ROW_TILE, d), lambda i: (i, 0)) for _ in out_dtypes],
        compiler_params=_params(("parallel",)),
        name="norm_modulate",
    )(x, g.reshape(1, d), mod)
    return outs


def _final_norm_kernel(x_ref, g_ref, o_ref):
    x = x_ref[...]
    o_ref[...] = x * lax.rsqrt(jnp.mean(x * x, axis=-1, keepdims=True) + EPS) * g_ref[...]


def final_norm(x, g):
    t, d = x.shape
    return pl.pallas_call(
        _final_norm_kernel,
        out_shape=jax.ShapeDtypeStruct((t, d), f32),
        grid=(t // ROW_TILE,),
        in_specs=[pl.BlockSpec((ROW_TILE, d), lambda i: (i, 0)),
                  pl.BlockSpec((1, d), lambda i: (0, 0))],
        out_specs=pl.BlockSpec((ROW_TILE, d), lambda i: (i, 0)),
        compiler_params=_params(("parallel",)),
        name="final_norm",
    )(x, g.reshape(1, d))


def _mla_prep_kernel(p_ref, qg_ref, kvg_ref, wq_ref, wkv_ref, cos_ref, sin_ref,
                     q_ref, kn_ref, kpe_ref, v_ref):
    def rms(v, g):
        return v * lax.rsqrt(jnp.mean(v * v, axis=-1, keepdims=True) + EPS) * g

    cos, sin = cos_ref[...], sin_ref[...]

    def rope(t):
        return t * cos + pltpu.roll(t, MLA_ROPE, axis=1) * sin

    qn = rms(p_ref[:, :Q_LORA], qg_ref[...]).astype(bf16)
    kvn = rms(p_ref[:, Q_LORA:Q_LORA + KV_LORA], kvg_ref[...]).astype(bf16)
    scale = (MLA_NOPE + MLA_ROPE) ** -0.5
    q = jnp.dot(qn, wq_ref[...], preferred_element_type=f32) * scale
    hw = 2 * LANES
    for h in range(MLA_HEADS):
        q_ref[:, h * hw:h * hw + LANES] = q[:, h * hw:h * hw + LANES].astype(bf16)
        q_ref[:, h * hw + LANES:(h + 1) * hw] = rope(q[:, h * hw + LANES:(h + 1) * hw]).astype(bf16)
    kv = jnp.dot(kvn, wkv_ref[...], preferred_element_type=f32)
    kn_ref[...] = kv[:, :MLA_HEADS * MLA_NOPE].astype(bf16)
    v_ref[...] = kv[:, MLA_HEADS * MLA_NOPE:].astype(bf16)
    kpe_ref[...] = rope(p_ref[:, Q_LORA + KV_LORA:]).astype(bf16)


def mla_prep(proj, qn_g, kvn_g, wq, wkv, cos_t, sin_t, tiles_per_batch):
    t = proj.shape[0]
    pw = proj.shape[1]
    qw = MLA_HEADS * 2 * LANES
    return pl.pallas_call(
        _mla_prep_kernel,
        out_shape=[jax.ShapeDtypeStruct((t, qw), bf16),
                   jax.ShapeDtypeStruct((t, MLA_HEADS * MLA_NOPE), bf16),
                   jax.ShapeDtypeStruct((t, LANES), bf16),
                   jax.ShapeDtypeStruct((t, MLA_HEADS * MLA_V), bf16)],
        grid=(t // ROW_TILE,),
        in_specs=[pl.BlockSpec((ROW_TILE, pw), lambda i: (i, 0)),
                  pl.BlockSpec((1, Q_LORA), lambda i: (0, 0)),
                  pl.BlockSpec((1, KV_LORA), lambda i: (0, 0)),
                  pl.BlockSpec(wq.shape, lambda i: (0, 0)),
                  pl.BlockSpec(wkv.shape, lambda i: (0, 0)),
                  pl.BlockSpec((ROW_TILE, LANES), lambda i: (i % tiles_per_batch, 0)),
                  pl.BlockSpec((ROW_TILE, LANES), lambda i: (i % tiles_per_batch, 0))],
        out_specs=[pl.BlockSpec((ROW_TILE, qw), lambda i: (i, 0)),
                   pl.BlockSpec((ROW_TILE, MLA_HEADS * MLA_NOPE), lambda i: (i, 0)),
                   pl.BlockSpec((ROW_TILE, LANES), lambda i: (i, 0)),
                   pl.BlockSpec((ROW_TILE, MLA_HEADS * MLA_V), lambda i: (i, 0))],
        compiler_params=_params(("parallel",)),
        name="mla_prep",
    )(proj, qn_g.reshape(1, -1), kvn_g.reshape(1, -1), wq, wkv, cos_t, sin_t)


def _mla_attn_kernel(q_ref, kn_ref, kpe_ref, v_ref, o_ref, *, lat_tiles, kv_tiles, tk):
    qi = pl.program_id(2)
    q = q_ref[...]
    tq = q.shape[0]
    first = jnp.where(qi < lat_tiles, 0, lat_tiles)

    def body(j, carry):
        m, l, acc = carry
        rows = pl.ds(pl.multiple_of(j * tk, tk), tk)
        k = jnp.concatenate([kn_ref[rows, :], kpe_ref[rows, :]], axis=-1)
        s = lax.dot_general(q, k, (((1,), (1,)), ((), ())), preferred_element_type=f32)
        m_new = jnp.maximum(m, jnp.max(s, axis=-1, keepdims=True))
        alpha = jnp.exp(m - m_new)
        p = jnp.exp(s - m_new)
        l = alpha * l + jnp.sum(p, axis=-1, keepdims=True)
        acc = alpha * acc + jnp.dot(p.astype(bf16), v_ref[rows, :], preferred_element_type=f32)
        return m_new, l, acc

    init = (jnp.full((tq, 1), NEG_BIG, f32), jnp.zeros((tq, 1), f32), jnp.zeros((tq, MLA_V), f32))
    _, l, acc = lax.fori_loop(first, kv_tiles, body, init)
    o_ref[...] = (acc / l).astype(o_ref.dtype)


def mla_attention(q, kn, kpe, v, n_batch, lat_tiles):
    p_len = q.shape[1]
    tq = tk = ROW_TILE
    kv_tiles = p_len // tk
    kern = functools.partial(_mla_attn_kernel, lat_tiles=lat_tiles, kv_tiles=kv_tiles, tk=tk)
    return pl.pallas_call(
        kern,
        out_shape=jax.ShapeDtypeStruct((n_batch, p_len, MLA_HEADS * MLA_V), bf16),
        grid=(n_batch, MLA_HEADS, p_len // tq),
        in_specs=[pl.BlockSpec((None, tq, 2 * LANES), lambda b, h, i: (b, i, h)),
                  pl.BlockSpec((None, p_len, MLA_NOPE), lambda b, h, i: (b, 0, h)),
                  pl.BlockSpec((None, p_len, LANES), lambda b, h, i: (b, 0, 0)),
                  pl.BlockSpec((None, p_len, MLA_V), lambda b, h, i: (b, 0, h))],
        out_specs=pl.BlockSpec((None, tq, MLA_V), lambda b, h, i: (b, i, h)),
        compiler_params=_params(("parallel", "parallel", "arbitrary")),
        name="mla_attention",
    )(q, kn, kpe, v)


NA_QROWS = 8
NA_BAND = 16


def _na_geometry(rows):
    assert rows % NA_QROWS == 0 and rows >= NA_BAND, rows
    nblk = rows // NA_QROWS
    starts = [min(max(NA_QROWS * i - WIN_R // 2, 0), rows - NA_BAND) for i in range(nblk)]
    if nblk <= 3:
        reps, pat = list(range(nblk)), list(range(nblk))
    else:
        reps = [0, 1, nblk - 1]
        pat = [0] + [1] * (nblk - 2) + [2]
        for i in range(1, nblk - 1):
            assert starts[i] == NA_QROWS * i - WIN_R // 2
    return nblk, starts, reps, pat


def na_bias_tables(rpb, rows):
    _, starts, reps, _ = _na_geometry(rows)
    wr = min(WIN_R, rows)
    tabs = []
    qc = jnp.arange(GRID_W)
    cs = jnp.clip(qc - WIN_C // 2, 0, GRID_W - WIN_C)
    kc = jnp.arange(GRID_W)
    col_ok = (kc[None, :] >= cs[:, None]) & (kc[None, :] < cs[:, None] + WIN_C)
    col_off = jnp.clip(kc[None, :] - qc[:, None] + (WIN_C - 1), 0, 2 * WIN_C - 2)
    for i in reps:
        qr = NA_QROWS * i + jnp.arange(NA_QROWS)
        kr = starts[i] + jnp.arange(NA_BAND)
        rs = jnp.clip(qr - wr // 2, 0, rows - wr)
        row_ok = (kr[None, :] >= rs[:, None]) & (kr[None, :] < rs[:, None] + wr)
        row_off = jnp.clip(kr[None, :] - qr[:, None] + (WIN_R - 1), 0, 2 * WIN_R - 2)
        b = rpb[:, row_off[:, None, :, None], col_off[None, :, None, :]]
        ok = row_ok[:, None, :, None] & col_ok[None, :, None, :]
        b = jnp.where(ok[None], b.astype(f32), NEG_BIG)
        tabs.append(b.reshape(rpb.shape[0], NA_QROWS * GRID_W, NA_BAND * GRID_W))
    return jnp.stack(tabs, axis=0)


def _na_kernel(q_ref, k_ref, v_ref, bias_ref, o_ref, *, n_lat, starts, pat):
    scale = NA_DIM ** -0.5
    k_ctx = k_ref[n_lat:, :]
    v_ctx = v_ref[n_lat:, :]
    nt = (((1,), (1,)), ((), ()))
    qt = NA_QROWS * GRID_W
    bt = NA_BAND * GRID_W
    for i, (st, pt) in enumerate(zip(starts, pat)):
        q = q_ref[i * qt:(i + 1) * qt, :]
        kb = k_ref[st * GRID_W:st * GRID_W + bt, :]
        vb = v_ref[st * GRID_W:st * GRID_W + bt, :]
        s_loc = lax.dot_general(q, kb, nt, preferred_element_type=f32) * scale + bias_ref[pt]
        s_ctx = lax.dot_general(q, k_ctx, nt, preferred_element_type=f32) * scale
        m = jnp.maximum(jnp.max(s_loc, axis=-1, keepdims=True), jnp.max(s_ctx, axis=-1, keepdims=True))
        p_loc = jnp.exp(s_loc - m)
        p_ctx = jnp.exp(s_ctx - m)
        l = jnp.sum(p_loc, axis=-1, keepdims=True) + jnp.sum(p_ctx, axis=-1, keepdims=True)
        o = (jnp.dot(p_loc.astype(bf16), vb, preferred_element_type=f32)
             + jnp.dot(p_ctx.astype(bf16), v_ctx, preferred_element_type=f32))
        o_ref[i * qt:(i + 1) * qt, :] = (o / l).astype(o_ref.dtype)
    qz = q_ref[n_lat:, :]
    s = lax.dot_general(qz, k_ctx, nt, preferred_element_type=f32) * scale
    p = jnp.exp(s - jnp.max(s, axis=-1, keepdims=True))
    o = jnp.dot(p.astype(bf16), v_ctx, preferred_element_type=f32) / jnp.sum(p, axis=-1, keepdims=True)
    o_ref[n_lat:, :] = o.astype(o_ref.dtype)


def neighbourhood_attention(qkv, bias, n_batch, n_lat):
    p_len = qkv.shape[1]
    rows = n_lat // GRID_W
    _, starts, _, pat = _na_geometry(rows)
    npat = bias.shape[0]
    kern = functools.partial(_na_kernel, n_lat=n_lat, starts=tuple(starts), pat=tuple(pat))
    hspec = lambda off: pl.BlockSpec((None, p_len, NA_DIM), lambda h, b: (b, 0, off + h))
    return pl.pallas_call(
        kern,
        out_shape=jax.ShapeDtypeStruct((n_batch, p_len, NA_HEADS * NA_DIM), bf16),
        grid=(NA_HEADS, n_batch),
        in_specs=[hspec(0), hspec(NA_HEADS), hspec(2 * NA_HEADS),
                  pl.BlockSpec((npat, None) + bias.shape[2:], lambda h, b: (0, h, 0, 0))],
        out_specs=pl.BlockSpec((None, p_len, NA_DIM), lambda h, b: (b, 0, h)),
        compiler_params=_params(("parallel", "parallel")),
        name="neighbourhood_attention",
    )(qkv, qkv, qkv, bias)


def _segment_conv(x, w_ref, n_lat):
    p_len = x.shape[0]
    row = lax.broadcasted_iota(jnp.int32, (p_len, 1), 0)
    local = jnp.where(row < n_lat, row, row - n_lat)
    seg_len = jnp.where(row < n_lat, n_lat, p_len - n_lat)
    y = x * w_ref[CONV_PAD_L:CONV_PAD_L + 1, :]
    for j in range(CONV_W):
        off = j - CONV_PAD_L
        if off == 0:
            continue
        shifted = pltpu.roll(x, (-off) % p_len, axis=0)
        ok = (local + off >= 0) & (local + off < seg_len)
        y = y + jnp.where(ok, shifted, 0.0) * w_ref[j:j + 1, :]
    return y


SCAN_CHUNK = 256


def _chunk_scan(a, b, reverse):
    n = a.shape[0]
    row = lax.broadcasted_iota(jnp.int32, (n, 1), 0)
    k = 1
    while k < n:
        if reverse:
            a_s = pltpu.roll(a, n - k, axis=0)
            b_s = pltpu.roll(b, n - k, axis=0)
            ok = row < n - k
        else:
            a_s = pltpu.roll(a, k, axis=0)
            b_s = pltpu.roll(b, k, axis=0)
            ok = row >= k
        b = b + a * jnp.where(ok, b_s, 0.0)
        a = a * jnp.where(ok, a_s, 1.0)
        k *= 2
    return a, b


def _lru_kernel(lu_ref, ly_ref, cw_ref, cb_ref, wa_ref, ba_ref, wx_ref, bx_ref, lam_ref, o_ref,
                u_scr, a_scr, b_scr, h_scr, *, n_lat):
    p_len = lu_ref.shape[0]
    u_scr[...] = _segment_conv(lu_ref[...], cw_ref, n_lat) + cb_ref[...]
    segments = ((n_lat, p_len - n_lat), (0, n_lat))

    for d, reverse in enumerate((False, True)):
        ub = u_scr[...].astype(bf16)
        r = _sigmoid(jnp.dot(ub, wa_ref[d].astype(bf16), preferred_element_type=f32) + ba_ref[d:d + 1, :])
        g = _sigmoid(jnp.dot(ub, wx_ref[d].astype(bf16), preferred_element_type=f32) + bx_ref[d:d + 1, :])
        log_a = (-LRU_C * r) * _softplus(-lam_ref[d:d + 1, :])
        a_scr[...] = jnp.exp(log_a)
        th = jnp.tanh(log_a)
        b_scr[...] = jnp.sqrt(-2.0 * th / (1.0 - th)) * (g * u_scr[...])

        state = jnp.zeros((1, LANES), f32)
        for start, length in segments:
            n_chunks = length // SCAN_CHUNK

            def body(c, h_prev, start=start, n_chunks=n_chunks, reverse=reverse):
                ci = (n_chunks - 1 - c) if reverse else c
                rows = pl.ds(pl.multiple_of(start + ci * SCAN_CHUNK, SCAN_CHUNK), SCAN_CHUNK)
                a_cum, h = _chunk_scan(a_scr[rows, :], b_scr[rows, :], reverse)
                h = h + a_cum * h_prev
                if d == 0:
                    h_scr[rows, :] = h
                else:
                    h_scr[rows, :] = h_scr[rows, :] + h
                return h[0:1, :] if reverse else h[SCAN_CHUNK - 1:SCAN_CHUNK, :]

            state = lax.fori_loop(0, n_chunks, body, state)

    o_ref[...] = (h_scr[...] * _gelu_tanh(ly_ref[...])).astype(o_ref.dtype)


def rglru(luy, conv_w, conv_b, wa, ba, wx, bx, lam, n_batch, n_lat):
    p_len = luy.shape[1]
    nblk = LRU_W // LANES
    assert LRU_W // LRU_BLOCKS == LANES and n_lat % SCAN_CHUNK == 0 and (p_len - n_lat) % SCAN_CHUNK == 0
    vec = lambda rows: pl.BlockSpec((rows, LANES), lambda b, j: (0, j))
    return pl.pallas_call(
        functools.partial(_lru_kernel, n_lat=n_lat),
        out_shape=jax.ShapeDtypeStruct((n_batch, p_len, LRU_W), bf16),
        grid=(n_batch, nblk),
        in_specs=[pl.BlockSpec((None, p_len, LANES), lambda b, j: (b, 0, j)),
                  pl.BlockSpec((None, p_len, LANES), lambda b, j: (b, 0, nblk + j)),
                  vec(CONV_W), vec(1),
                  pl.BlockSpec((2, None, LANES, LANES), lambda b, j: (0, j, 0, 0)), vec(2),
                  pl.BlockSpec((2, None, LANES, LANES), lambda b, j: (0, j, 0, 0)), vec(2),
                  vec(2)],
        out_specs=pl.BlockSpec((None, p_len, LANES), lambda b, j: (b, 0, j)),
        scratch_shapes=[pltpu.VMEM((p_len, LANES), f32) for _ in range(4)],
        compiler_params=_params(("parallel", "parallel")),
        name="rglru",
    )(luy, luy, conv_w, conv_b.reshape(1, LRU_W), wa, ba, wx, bx, lam)


def _dn_pre_kernel(x_ref, w_ref, o_ref, *, n_lat):
    j = pl.program_id(1)
    u = _silu(_segment_conv(x_ref[...], w_ref, n_lat))
    nrm = u * lax.rsqrt(jnp.sum(u * u, axis=-1, keepdims=True) + EPS)
    qk_scale = jnp.where(j < DN_HEADS, DN_DK ** -0.5, 1.0)
    o_ref[...] = jnp.where(j < 2 * DN_HEADS, nrm * qk_scale, u)


def deltanet_pre(dqkv, conv_w, n_batch, n_lat, col_blocks):
    p_len = dqkv.shape[1]
    return pl.pallas_call(
        functools.partial(_dn_pre_kernel, n_lat=n_lat),
        out_shape=jax.ShapeDtypeStruct((n_batch, p_len, col_blocks * LANES), f32),
        grid=(n_batch, col_blocks),
        in_specs=[pl.BlockSpec((None, p_len, LANES), lambda b, j: (b, 0, j)),
                  pl.BlockSpec((CONV_W, LANES), lambda b, j: (0, j))],
        out_specs=pl.BlockSpec((None, p_len, LANES), lambda b, j: (b, 0, j)),
        compiler_params=_params(("parallel", "parallel")),
        name="deltanet_pre",
    )(dqkv, conv_w)


def _dn_gate_kernel(ab_ref, alog_ref, dt_ref, o_ref):
    nh2 = 2 * DN_HEADS
    a_raw = ab_ref[:, :nh2]
    b_raw = ab_ref[:, nh2:2 * nh2]
    g = -jnp.exp(alog_ref[...]) * _softplus(a_raw + dt_ref[...])
    beta = _sigmoid(b_raw)
    n = g.shape[0]
    r = lax.broadcasted_iota(jnp.int32, (n, n), 0)
    c = lax.broadcasted_iota(jnp.int32, (n, n), 1)
    same = (r // DN_CHUNK) == (c // DN_CHUNK)
    hi = lax.Precision.HIGHEST
    pre = jnp.dot(jnp.where(same & (c <= r), 1.0, 0.0), g, preferred_element_type=f32, precision=hi)
    suf = jnp.dot(jnp.where(same & (c >= r), 1.0, 0.0), g, preferred_element_type=f32, precision=hi)
    tot = jnp.dot(jnp.where(same, 1.0, 0.0), g, preferred_element_type=f32, precision=hi)
    pad = jnp.zeros((n, LANES - 3 * nh2), f32)
    o_ref[...] = jnp.concatenate([pre[:, :DN_HEADS], suf[:, DN_HEADS:], beta, tot, pad], axis=-1)


def deltanet_gates(ab, a_log, dt_bias):
    t = ab.shape[0]
    nh2 = 2 * DN_HEADS
    return pl.pallas_call(
        _dn_gate_kernel,
        out_shape=jax.ShapeDtypeStruct((t, LANES), f32),
        grid=(t // ROW_TILE,),
        in_specs=[pl.BlockSpec((ROW_TILE, LANES), lambda i: (i, 0)),
                  pl.BlockSpec((1, nh2), lambda i: (0, 0)),
                  pl.BlockSpec((1, nh2), lambda i: (0, 0))],
        out_specs=pl.BlockSpec((ROW_TILE, LANES), lambda i: (i, 0)),
        compiler_params=_params(("parallel",)),
        name="deltanet_gates",
    )(ab, a_log.reshape(1, nh2), dt_bias.reshape(1, nh2))


def _tri_inverse(low):
    c = low.shape[0]
    hi = lax.Precision.HIGHEST
    eye = jnp.where(lax.broadcasted_iota(jnp.int32, (c, c), 0) == lax.broadcasted_iota(jnp.int32, (c, c), 1),
                    1.0, 0.0)
    x = -low
    inv = eye + x
    k = 2
    while k < c:
        x = jnp.dot(x, x, preferred_element_type=f32, precision=hi)
        inv = inv + jnp.dot(x, inv, preferred_element_type=f32, precision=hi)
        k *= 2
    return inv


def _dn_kernel(q_ref, k_ref, v_ref, z_ref, gcol_ref, grow_ref, ng_ref, o_ref, acc_scr, *, n_lat):
    p_len = q_ref.shape[0]
    cs = DN_CHUNK
    ri = lax.broadcasted_iota(jnp.int32, (cs, cs), 0)
    ci = lax.broadcasted_iota(jnp.int32, (cs, cs), 1)
    nt = (((1,), (1,)), ((), ()))
    tn = (((0,), (0,)), ((), ()))
    segments = ((n_lat, p_len - n_lat), (0, n_lat))

    for d, reverse in enumerate((False, True)):
        incl = (ci >= ri) if reverse else (ci <= ri)
        strict = (ci > ri) if reverse else (ci < ri)
        state = jnp.zeros((DN_DK, DN_DV), f32)
        for start, length in segments:
            n_chunks = length // cs

            def body(c, s, start=start, n_chunks=n_chunks, d=d, reverse=reverse, incl=incl, strict=strict):
                cidx = (n_chunks - 1 - c) if reverse else c
                r0 = pl.multiple_of(start + cidx * cs, cs)
                rows = pl.ds(r0, cs)
                q = q_ref[rows, :]
                k = k_ref[rows, :]
                v = v_ref[rows, :]
                gc = gcol_ref[rows, d:d + 1]
                beta = gcol_ref[rows, 2 + d:3 + d]
                gtot = gcol_ref[rows, 4 + d:5 + d]
                gr = grow_ref[d, pl.ds(r0 // cs, 1), :]
                decay = jnp.where(incl, jnp.exp(jnp.where(incl, gc - gr, 0.0)), 0.0)
                kb = k * beta
                kk = lax.dot_general(kb, k, nt, preferred_element_type=f32)
                t_inv = _tri_inverse(jnp.where(strict, kk * decay, 0.0))
                e_gc = jnp.exp(gc)
                w_v = jnp.dot(t_inv, v * beta, preferred_element_type=f32)
                w_k = jnp.dot(t_inv, kb * e_gc, preferred_element_type=f32)
                a_in = lax.dot_general(q, k, nt, preferred_element_type=f32) * decay
                v_new = w_v - jnp.dot(w_k, s, preferred_element_type=f32)
                o = (jnp.dot(q * e_gc, s, preferred_element_type=f32)
                     + jnp.dot(a_in, v_new, preferred_element_type=f32))
                if d == 0:
                    acc_scr[rows, :] = o
                else:
                    acc_scr[rows, :] = acc_scr[rows, :] + o
                k_dec = k * jnp.exp(gtot - gc)
                g_end = jnp.exp(gtot[0:1, :])
                return s * g_end + lax.dot_general(k_dec, v_new, tn, preferred_element_type=f32)

            state = lax.fori_loop(0, n_chunks, body, state)

    o = acc_scr[...]
    y = o * lax.rsqrt(jnp.mean(o * o, axis=-1, keepdims=True) + EPS) * ng_ref[...]
    o_ref[...] = (y * _silu(z_ref[...])).astype(o_ref.dtype)


def deltanet(qkv, dz_src, z_col0, gcol, grow, norm_g, n_batch, n_lat):
    p_len = qkv.shape[1]
    hspec = lambda off: pl.BlockSpec((None, p_len, LANES), lambda b, h: (b, 0, off + h))
    return pl.pallas_call(
        functools.partial(_dn_kernel, n_lat=n_lat),
        out_shape=jax.ShapeDtypeStruct((n_batch, p_len, DN_HEADS * DN_DV), bf16),
        grid=(n_batch, DN_HEADS),
        in_specs=[hspec(0), hspec(DN_HEADS), hspec(2 * DN_HEADS),
                  pl.BlockSpec((None, p_len, LANES), lambda b, h: (b, 0, z_col0 + h)),
                  pl.BlockSpec((None, None, p_len, 8), lambda b, h: (b, h, 0, 0)),
                  pl.BlockSpec((None, None, 2, p_len // DN_CHUNK, DN_CHUNK), lambda b, h: (b, h, 0, 0, 0)),
                  pl.BlockSpec((1, DN_DV), lambda b, h: (0, 0))],
        out_specs=pl.BlockSpec((None, p_len, LANES), lambda b, h: (b, 0, h)),
        scratch_shapes=[pltpu.VMEM((p_len, DN_DV), f32)],
        compiler_params=_params(("parallel", "parallel")),
        name="deltanet",
    )(qkv, qkv, qkv, dz_src, gcol, grow, norm_g.reshape(1, DN_DV))


def _merge_kernel(*refs):
    o_refs, g_refs, wb_ref, out_ref = refs[:N_BRANCH], refs[N_BRANCH:2 * N_BRANCH], refs[-2], refs[-1]
    acc = None
    for i in range(N_BRANCH):
        y = jnp.dot(o_refs[i][...], wb_ref[i], preferred_element_type=f32)
        y = _sigmoid(g_refs[i][...].astype(f32)) * y
        acc = y if acc is None else acc + y
    out_ref[...] = acc.astype(out_ref.dtype)


def merge_branches(branches, gates, w_branch):
    t, bw = branches[0].shape
    d = w_branch.shape[2]
    tm, tn = ROW_TILE * 2, min(1024, d)
    while t % tm:
        tm //= 2
    nj = d // tn
    bspec = pl.BlockSpec((tm, bw), lambda j, i: (i, 0))
    gspecs = [pl.BlockSpec((tm, tn), functools.partial(lambda j, i, br: (i, br * nj + j), br=br))
              for br in range(N_BRANCH)]
    return pl.pallas_call(
        _merge_kernel,
        out_shape=jax.ShapeDtypeStruct((t, d), bf16),
        grid=(nj, t // tm),
        in_specs=[bspec] * N_BRANCH + gspecs + [pl.BlockSpec((N_BRANCH, bw, tn), lambda j, i: (0, 0, j))],
        out_specs=pl.BlockSpec((tm, tn), lambda j, i: (i, j)),
        compiler_params=_params(("parallel", "parallel")),
        name="merge_branches",
    )(*branches, *([gates] * N_BRANCH), w_branch)


def _out_proj_kernel(m_ref, w_ref, x_ref, mod_ref, o_ref, *, gate_row):
    y = jnp.dot(m_ref[...], w_ref[...], preferred_element_type=f32)
    o_ref[...] = x_ref[...] + mod_ref[gate_row:gate_row + 1, :] * y


def out_proj_residual(merged, w_out, x, mod, gate_row, geom):
    t, d = x.shape
    n_batch, tpb, lat = geom
    mrow = functools.partial(_mod_row, tiles_per_batch=tpb, lat_tiles=lat, n_batch=n_batch)
    return pl.pallas_call(
        functools.partial(_out_proj_kernel, gate_row=gate_row),
        out_shape=jax.ShapeDtypeStruct((t, d), f32),
        grid=(t // ROW_TILE,),
        in_specs=[pl.BlockSpec((ROW_TILE, d), lambda i: (i, 0)),
                  pl.BlockSpec((d, d), lambda i: (0, 0)),
                  pl.BlockSpec((ROW_TILE, d), lambda i: (i, 0)),
                  pl.BlockSpec((None, N_MOD, d), lambda i: (mrow(i), 0, 0))],
        out_specs=pl.BlockSpec((ROW_TILE, d), lambda i: (i, 0)),
        compiler_params=_params(("parallel",)),
        name="out_proj_residual",
    )(merged, w_out, x, mod)


def _route_kernel(h_ref, rw_ref, rb_ref, sel_ref, cnt_ref, carry):
    i = pl.program_id(0)

    @pl.when(i == 0)
    def _():
        carry[...] = jnp.zeros_like(carry)

    logits = jnp.dot(h_ref[...], rw_ref[...], preferred_element_type=f32,
                     precision=lax.Precision.HIGHEST) + rb_ref[...]
    n = logits.shape[0]
    lane = lax.broadcasted_iota(jnp.int32, (n, LANES), 1).astype(f32)
    work = logits
    vals, idxs, hots = [], [], []
    for _ in range(TOP_K):
        m = jnp.max(work, axis=-1, keepdims=True)
        idx = jnp.min(jnp.where(work == m, lane, float(LANES)), axis=-1, keepdims=True)
        hot = lane == idx
        vals.append(m)
        idxs.append(idx)
        hots.append(hot)
        work = jnp.where(hot, NEG_BIG * 2.0, work)
    exps = [jnp.exp(v - vals[0]) for v in vals]
    den = exps[0]
    for e in exps[1:]:
        den = den + e
    onehot = jnp.zeros((n, LANES), f32)
    for hot in hots:
        onehot = onehot + jnp.where(hot, 1.0, 0.0)
    r = lax.broadcasted_iota(jnp.int32, (n, n), 0)
    c = lax.broadcasted_iota(jnp.int32, (n, n), 1)
    before = jnp.dot(jnp.where(c < r, 1.0, 0.0).astype(bf16), onehot.astype(bf16),
                     preferred_element_type=f32) + carry[...]
    out = jnp.zeros((n, LANES), f32)
    for kk in range(TOP_K):
        rank = jnp.sum(jnp.where(hots[kk], before, 0.0), axis=-1, keepdims=True)
        out = jnp.where(lane == float(kk), idxs[kk], out)
        out = jnp.where(lane == float(TOP_K + kk), rank, out)
        out = jnp.where(lane == float(2 * TOP_K + kk), exps[kk] / den, out)
    sel_ref[...] = out
    carry[...] = carry[...] + jnp.sum(onehot, axis=0, keepdims=True)
    cnt_ref[...] = carry[...]


def moe_route(h, router_w, router_b):
    t, d = h.shape
    e = router_w.shape[1]
    rw = jnp.zeros((d, LANES), f32).at[:, :e].set(router_w)
    rb = jnp.full((1, LANES), NEG_BIG, f32).at[0, :e].set(router_b)
    return pl.pallas_call(
        _route_kernel,
        out_shape=[jax.ShapeDtypeStruct((t, LANES), f32), jax.ShapeDtypeStruct((1, LANES), f32)],
        grid=(t // ROW_TILE,),
        in_specs=[pl.BlockSpec((ROW_TILE, d), lambda i: (i, 0)),
                  pl.BlockSpec((d, LANES), lambda i: (0, 0)),
                  pl.BlockSpec((1, LANES), lambda i: (0, 0))],
        out_specs=[pl.BlockSpec((ROW_TILE, LANES), lambda i: (i, 0)),
                   pl.BlockSpec((1, LANES), lambda i: (0, 0))],
        scratch_shapes=[pltpu.VMEM((1, LANES), f32)],
        compiler_params=_params(("arbitrary",)),
        name="moe_route",
    )(h, rw, rb)


def _dispatch_kernel(pos_hbm, x_ref, init_ref, xs_hbm, pos_smem, sem_pos, sem_rows):
    del init_ref
    i = pl.program_id(0)
    n_sel = ROW_TILE * TOP_K
    cp = pltpu.make_async_copy(pos_hbm.at[i], pos_smem, sem_pos)
    cp.start()
    cp.wait()

    def row_copy(j):
        return pltpu.make_async_copy(x_ref.at[pl.ds(j // TOP_K, 1)], xs_hbm.at[pl.ds(pos_smem[j], 1)], sem_rows)

    def issue(j, c):
        row_copy(j).start()
        return c

    lax.fori_loop(0, n_sel, issue, 0)

    def drain(j, c):
        row_copy(j).wait()
        return c

    lax.fori_loop(0, n_sel, drain, 0)


def moe_dispatch(h, pos, n_slots):
    t, d = h.shape
    tiles = t // ROW_TILE
    init = jnp.zeros((n_slots, d), f32)
    return pl.pallas_call(
        _dispatch_kernel,
        out_shape=jax.ShapeDtypeStruct((n_slots, d), f32),
        grid=(tiles,),
        in_specs=[pl.BlockSpec(memory_space=pl.ANY),
                  pl.BlockSpec((ROW_TILE, d), lambda i: (i, 0)),
                  pl.BlockSpec(memory_space=pl.ANY)],
        out_specs=pl.BlockSpec(memory_space=pl.ANY),
        scratch_shapes=[pltpu.SMEM((ROW_TILE * TOP_K,), jnp.int32),
                        pltpu.SemaphoreType.DMA, pltpu.SemaphoreType.DMA],
        input_output_aliases={2: 0},
        compiler_params=_params(("arbitrary",)),
        name="moe_dispatch",
    )(pos.reshape(tiles, ROW_TILE * TOP_K), h, init)


def _expert_kernel(te_ref, xs_ref, w1g_ref, w1l_ref, b1g_ref, b1l_ref, w2_ref, b2_ref, ys_ref):
    del te_ref
    x = xs_ref[...].astype(bf16)
    glu = jnp.dot(x, w1g_ref[...], preferred_element_type=f32) + b1g_ref[...]
    lin = jnp.dot(x, w1l_ref[...], preferred_element_type=f32) + b1l_ref[...]
    glu = jnp.minimum(glu, SWIGLU_LIMIT)
    lin = jnp.clip(lin, -SWIGLU_LIMIT, SWIGLU_LIMIT)
    act = glu * _sigmoid(SWIGLU_ALPHA * glu) * (lin + 1.0)
    ys_ref[...] = jnp.dot(act.astype(bf16), w2_ref[...], preferred_element_type=f32) + b2_ref[...]


def moe_experts(xs, tile_expert, w1g, w1l, b1g, b1l, w2, b2):
    s, d = xs.shape
    ff = w1g.shape[2]
    grid_spec = pltpu.PrefetchScalarGridSpec(
        num_scalar_prefetch=1,
        grid=(s // ROW_TILE,),
        in_specs=[pl.BlockSpec((ROW_TILE, d), lambda i, te: (i, 0)),
                  pl.BlockSpec((None, d, ff), lambda i, te: (te[i], 0, 0)),
                  pl.BlockSpec((None, d, ff), lambda i, te: (te[i], 0, 0)),
                  pl.BlockSpec((None, 1, ff), lambda i, te: (te[i], 0, 0)),
                  pl.BlockSpec((None, 1, ff), lambda i, te: (te[i], 0, 0)),
                  pl.BlockSpec((None, ff, d), lambda i, te: (te[i], 0, 0)),
                  pl.BlockSpec((None, 1, d), lambda i, te: (te[i], 0, 0))],
        out_specs=pl.BlockSpec((ROW_TILE, d), lambda i, te: (i, 0)),
    )
    return pl.pallas_call(
        _expert_kernel,
        out_shape=jax.ShapeDtypeStruct((s, d), f32),
        grid_spec=grid_spec,
        compiler_params=_params(("arbitrary",)),
        name="moe_experts",
    )(tile_expert, xs, w1g, w1l, b1g, b1l, w2, b2)


def _combine_kernel(pos_hbm, ys_hbm, x_ref, wt_ref, mod_ref, o_ref, buf, pos_smem, sem_pos, sem_rows, *, gate_row):
    i = pl.program_id(0)
    n_sel = ROW_TILE * TOP_K
    cp = pltpu.make_async_copy(pos_hbm.at[i], pos_smem, sem_pos)
    cp.start()
    cp.wait()

    def row_copy(j):
        return pltpu.make_async_copy(ys_hbm.at[pl.ds(pos_smem[j], 1)],
                                     buf.at[j % TOP_K, pl.ds(j // TOP_K, 1)], sem_rows)

    def issue(j, c):
        row_copy(j).start()
        return c

    lax.fori_loop(0, n_sel, issue, 0)

    def drain(j, c):
        row_copy(j).wait()
        return c

    lax.fori_loop(0, n_sel, drain, 0)
    acc = wt_ref[:, 2 * TOP_K:2 * TOP_K + 1] * buf[0]
    for kk in range(1, TOP_K):
        acc = acc + wt_ref[:, 2 * TOP_K + kk:2 * TOP_K + kk + 1] * buf[kk]
    o_ref[...] = x_ref[...] + mod_ref[gate_row:gate_row + 1, :] * acc


def moe_combine(ys, pos, sel, x, mod, gate_row, geom):
    t, d = x.shape
    tiles = t // ROW_TILE
    n_batch, tpb, lat = geom
    mrow = functools.partial(_mod_row, tiles_per_batch=tpb, lat_tiles=lat, n_batch=n_batch)
    return pl.pallas_call(
        functools.partial(_combine_kernel, gate_row=gate_row),
        out_shape=jax.ShapeDtypeStruct((t, d), f32),
        grid=(tiles,),
        in_specs=[pl.BlockSpec(memory_space=pl.ANY),
                  pl.BlockSpec(memory_space=pl.ANY),
                  pl.BlockSpec((ROW_TILE, d), lambda i: (i, 0)),
                  pl.BlockSpec((ROW_TILE, LANES), lambda i: (i, 0)),
                  pl.BlockSpec((None, N_MOD, d), lambda i: (mrow(i), 0, 0))],
        out_specs=pl.BlockSpec((ROW_TILE, d), lambda i: (i, 0)),
        scratch_shapes=[pltpu.VMEM((TOP_K, ROW_TILE, d), f32),
                        pltpu.SMEM((ROW_TILE * TOP_K,), jnp.int32),
                        pltpu.SemaphoreType.DMA, pltpu.SemaphoreType.DMA],
        compiler_params=_params(("arbitrary",)),
        name="moe_combine",
    )(pos.reshape(tiles, ROW_TILE * TOP_K), ys, x, sel, mod)


def moe_ffn_residual(h, x, mod, gate_row, geom, router_w, router_b, w1, b1, w2, b2):
    t, d = h.shape
    n_exp = router_w.shape[1]
    sel, counts = moe_route(h, router_w, router_b)
    eidx = sel[:, :TOP_K].astype(jnp.int32)
    rank = sel[:, TOP_K:2 * TOP_K].astype(jnp.int32)
    counts = counts[0, :n_exp].astype(jnp.int32)
    padded = ((counts + ROW_TILE - 1) // ROW_TILE) * ROW_TILE
    ends = jnp.cumsum(padded)
    starts = ends - padded
    pos = starts[eidx] + rank
    n_tiles = (t * TOP_K) // ROW_TILE + n_exp
    tile_expert = jnp.minimum(jnp.searchsorted(ends, jnp.arange(n_tiles) * ROW_TILE, side="right"),
                              n_exp - 1).astype(jnp.int32)
    xs = moe_dispatch(h, pos, n_tiles * ROW_TILE)
    w1g, w1l = w1[:, :, 0::2].astype(bf16), w1[:, :, 1::2].astype(bf16)
    b1g, b1l = b1[:, None, 0::2], b1[:, None, 1::2]
    ys = moe_experts(xs, tile_expert, w1g, w1l, b1g, b1l, w2.astype(bf16), b2[:, None, :])
    return moe_combine(ys, pos, sel, x, mod, gate_row, geom)


def _rope_perm(w):
    lead = w.shape[:-1]
    nf = MLA_ROPE // 4
    wr = w.reshape(*lead, 2, 2, nf)
    return jnp.stack([-wr[..., 1, :], wr[..., 0, :]], axis=-2).reshape(*lead, MLA_ROPE)


def _split_w_in(w_in, d_model):
    sizes = (Q_LORA, KV_LORA, MLA_ROPE, NA_HEADS * NA_DIM, NA_HEADS * NA_DIM, NA_HEADS * NA_DIM, LRU_W, LRU_W,
             DN_HEADS * (2 * DN_DK + DN_DV), DN_HEADS * DN_DV, 2 * DN_HEADS, 2 * DN_HEADS, N_BRANCH * d_model)
    parts, s = [], 0
    for n in sizes:
        parts.append(w_in[:, s:s + n])
        s += n
    assert s == w_in.shape[1]
    return parts


def rope_tables(n_lat, p_len):
    t = jnp.arange(n_lat)
    row = (t // GRID_W).astype(f32)
    col = (t % GRID_W).astype(f32)
    nf = MLA_ROPE // 4
    inv_freq = ROPE_BASE ** (-jnp.arange(nf, dtype=f32) / nf)
    ang = jnp.concatenate([row[:, None] * inv_freq] * 2 + [col[:, None] * inv_freq] * 2, axis=1)
    zeros = jnp.zeros((n_lat, LANES - MLA_ROPE), f32)
    cos_t = jnp.concatenate([jnp.cos(ang), zeros], axis=1)
    sin_t = jnp.concatenate([jnp.sin(ang), zeros], axis=1)
    n_ctx = p_len - n_lat
    ctx_cos = jnp.concatenate([jnp.ones((n_ctx, MLA_ROPE), f32), jnp.zeros((n_ctx, LANES - MLA_ROPE), f32)], axis=1)
    return (jnp.concatenate([cos_t, ctx_cos], axis=0),
            jnp.concatenate([sin_t, jnp.zeros((n_ctx, LANES), f32)], axis=0))


def kernel(x, c, ctx, c_ctx, ada_w, ada_b, norm_mix_g, norm_ffn_g, w_in, mla_qn_g, mla_wq_up, mla_kvn_g, mla_wkv_up, na_rpb, lru_conv_w, lru_conv_b, lru_wa, lru_ba, lru_wx, lru_bx, lru_lam, dn_conv_w, dn_a_log, dn_dt_bias, dn_norm_g, w_branch, w_out, router_w, router_b, exp_w1, exp_b1, exp_w2, exp_b2, final_g):
    n_batch, n_lat, d = x.shape
    n_ctx = ctx.shape[1]
    depth = ada_w.shape[0]
    p_len = n_lat + n_ctx
    t = n_batch * p_len
    assert n_lat % ROW_TILE == 0 and n_ctx % ROW_TILE == 0 and n_lat % GRID_W == 0
    tpb, lat_tiles = p_len // ROW_TILE, n_lat // ROW_TILE
    geom = (n_batch, tpb, lat_tiles)

    stream = jnp.concatenate([x, ctx], axis=1).reshape(t, d)
    pad_rows = (-(n_batch + 1)) % 8
    cond = jnp.concatenate([c, c_ctx[None, :], jnp.zeros((pad_rows, d), f32)], axis=0)
    mods = ada_modulation(cond, ada_w, ada_b)[:, :n_batch + 1].reshape(depth, n_batch + 1, N_MOD, d)
    cos_t, sin_t = rope_tables(n_lat, p_len)

    for l in range(depth):
        mod = mods[l]
        (w_qc, w_kvc, w_kpe, w_naq, w_nak, w_nav, w_lu, w_ly, w_dqkv, w_dz, w_da, w_db, w_gt) = _split_w_in(w_in[l], d)
        w_mla = jnp.concatenate([w_qc, w_kvc, w_kpe, _rope_perm(w_kpe)], axis=1).astype(bf16)
        w_na = jnp.concatenate([w_naq, w_nak, w_nav], axis=1).astype(bf16)
        w_lru = jnp.concatenate([w_lu, w_ly], axis=1).astype(bf16)
        w_dn = jnp.concatenate([w_dqkv, w_dz], axis=1).astype(bf16)
        w_dg = jnp.concatenate([w_da, w_db, jnp.zeros((d, LANES - 4 * DN_HEADS), f32)], axis=1).astype(bf16)
        w_gt = w_gt.astype(bf16)
        wq = mla_wq_up[l].reshape(Q_LORA, MLA_HEADS, MLA_NOPE + MLA_ROPE)
        wq = jnp.concatenate([wq, _rope_perm(wq[..., MLA_NOPE:])], axis=-1).reshape(Q_LORA, -1).astype(bf16)
        wkv = mla_wkv_up[l].reshape(KV_LORA, MLA_HEADS, MLA_NOPE + MLA_V)
        wkv = jnp.concatenate([wkv[..., :MLA_NOPE].reshape(KV_LORA, -1),
                               wkv[..., MLA_NOPE:].reshape(KV_LORA, -1)], axis=1).astype(bf16)

        (h,) = norm_modulate(stream, norm_mix_g[l], mod, 0, geom, (bf16,))

        p_mla = matmul(h, w_mla, f32, 1024, 384, "proj_mla")
        q, kn, kpe, v = mla_prep(p_mla, mla_qn_g[l], mla_kvn_g[l], wq, wkv, cos_t, sin_t, tpb)
        o_a = mla_attention(q.reshape(n_batch, p_len, -1), kn.reshape(n_batch, p_len, -1),
                            kpe.reshape(n_batch, p_len, -1), v.reshape(n_batch, p_len, -1), n_batch, lat_tiles)

        p_na = matmul(h, w_na, bf16, 1024, 512, "proj_na")
        o_b = neighbourhood_attention(p_na.reshape(n_batch, p_len, -1), na_bias_tables(na_rpb[l], n_lat // GRID_W),
                                      n_batch, n_lat)

        p_lru = matmul(h, w_lru, f32, 1024, 512, "proj_lru")
        o_c = rglru(p_lru.reshape(n_batch, p_len, -1), lru_conv_w[l], lru_conv_b[l], lru_wa[l], lru_ba[l],
                    lru_wx[l], lru_bx[l], lru_lam[l], n_batch, n_lat)

        p_dn = matmul(h, w_dn, f32, 1024, 512, "proj_dn").reshape(n_batch, p_len, -1)
        qkv_dn = deltanet_pre(p_dn, dn_conv_w[l], n_batch, n_lat, 3 * DN_HEADS)
        gates = deltanet_gates(matmul(h, w_dg, f32, 1024, LANES, "proj_dn_gates"), dn_a_log[l], dn_dt_bias[l])
        gates = gates.reshape(n_batch, p_len, LANES)
        nh = DN_HEADS

        def per_head(cols):
            return jnp.transpose(cols.reshape(n_batch, p_len, 2, nh), (0, 3, 1, 2))

        gc_cols = per_head(gates[..., :2 * nh])
        gcol = jnp.concatenate([gc_cols, per_head(gates[..., 2 * nh:4 * nh]), per_head(gates[..., 4 * nh:6 * nh]),
                                jnp.zeros((n_batch, nh, p_len, 2), f32)], axis=-1)
        grow = jnp.transpose(gc_cols, (0, 1, 3, 2)).reshape(n_batch, nh, 2, p_len // DN_CHUNK, DN_CHUNK)
        o_d = deltanet(qkv_dn, p_dn, 3 * DN_HEADS, gcol, grow, dn_norm_g[l], n_batch, n_lat)

        gate_logits = matmul(h, w_gt, bf16, 1024, 512, "proj_gate")
        merged = merge_branches((o_a.reshape(t, -1), o_b.reshape(t, -1), o_c.reshape(t, -1), o_d.reshape(t, -1)),
                                gate_logits, w_branch[l].astype(bf16))
        stream = out_proj_residual(merged, w_out[l].astype(bf16), stream, mod, 2, geom)

        (h2,) = norm_modulate(stream, norm_ffn_g[l], mod, 3, geom, (f32,))
        stream = moe_ffn_residual(h2, stream, mod, 5, geom, router_w[l], router_b[l],
                                  exp_w1[l], exp_b1[l], exp_w2[l], exp_b2[l])

    out = final_norm(stream, final_g)
    return out.reshape(n_batch, p_len, d)[:, :n_lat]
```

```python
import functools
import math

import jax
import jax.numpy as jnp
from jax import lax
from jax.experimental import pallas as pl
from jax.experimental.pallas import tpu as pltpu

GRID_W = 64
EPS = 1e-6
N_MOD = 6
N_BRANCH = 4

MLA_HEADS = 8
MLA_NOPE = 128
MLA_ROPE = 64
MLA_V = 128
Q_LORA = 512
KV_LORA = 512
ROPE_BASE = 10000.0

NA_HEADS = 8
NA_DIM = 128
WIN_R = 8
WIN_C = 16

LRU_W = 1024
LRU_BLOCKS = 8
LRU_C = 8.0
CONV_W = 4
CONV_PAD_L = 2

DN_HEADS = 8
DN_DK = 128
DN_DV = 128
DN_CHUNK = 64

N_EXPERTS = 32
TOP_K = 4
D_FF = 640
SWIGLU_ALPHA = 1.702
SWIGLU_LIMIT = 7.0

ROW_TILE = 256
LANES = 128
NEG_BIG = -1e30
VMEM_LIMIT = 56 * 1024 * 1024

f32 = jnp.float32
bf16 = jnp.bfloat16


def _params(sem, vmem=VMEM_LIMIT):
    return pltpu.CompilerParams(dimension_semantics=sem, vmem_limit_bytes=vmem)


def _sigmoid(x):
    return 1.0 / (1.0 + jnp.exp(-x))


def _silu(x):
    return x * _sigmoid(x)


def _softplus(x):
    return jnp.maximum(x, 0.0) + jnp.log1p(jnp.exp(-jnp.abs(x)))


def _gelu_tanh(x):
    return 0.5 * x * (1.0 + jnp.tanh(math.sqrt(2.0 / math.pi) * (x + 0.044715 * x * x * x)))


def _mm_kernel(x_ref, w_ref, o_ref):
    o_ref[...] = jnp.dot(x_ref[...], w_ref[...], preferred_element_type=f32).astype(o_ref.dtype)


def matmul(x, w, layer, col0, n, out_dtype, tm, tn, name):
    m, k = x.shape
    while m % tm:
        tm //= 2
    tn = min(tn, n)
    assert tm % 8 == 0 and n % tn == 0 and col0 % tn == 0, (m, n, tm, tn, col0)
    cb = col0 // tn
    return pl.pallas_call(
        _mm_kernel,
        out_shape=jax.ShapeDtypeStruct((m, n), out_dtype),
        grid=(n // tn, m // tm),
        in_specs=[pl.BlockSpec((tm, k), lambda j, i: (i, 0)),
                  pl.BlockSpec((None, k, tn), lambda j, i: (layer, 0, cb + j))],
        out_specs=pl.BlockSpec((tm, tn), lambda j, i: (i, j)),
        compiler_params=_params(("parallel", "parallel")),
        name=name,
    )(x, w)


def _ada_kernel(c_ref, w_ref, b_ref, o_ref):
    a = _silu(c_ref[...])
    o_ref[...] = jnp.dot(a, w_ref[...], preferred_element_type=f32,
                         precision=lax.Precision.HIGHEST) + b_ref[...]


def ada_modulation(cond, ada_w, ada_b):
    depth, d, n = ada_w.shape
    r = cond.shape[0]
    tn = 512
    return pl.pallas_call(
        _ada_kernel,
        out_shape=jax.ShapeDtypeStruct((depth, r, n), f32),
        grid=(depth, n // tn),
        in_specs=[pl.BlockSpec((r, d), lambda l, j: (0, 0)),
                  pl.BlockSpec((None, d, tn), lambda l, j: (l, 0, j)),
                  pl.BlockSpec((None, 1, tn), lambda l, j: (l, 0, j))],
        out_specs=pl.BlockSpec((None, r, tn), lambda l, j: (l, 0, j)),
        compiler_params=_params(("parallel", "parallel")),
        name="ada_modulation",
    )(cond, ada_w, ada_b.reshape(depth, 1, n))


def _mod_row(i, tiles_per_batch, lat_tiles, n_batch):
    return jnp.where(i % tiles_per_batch < lat_tiles, i // tiles_per_batch, n_batch)


def _norm_mod_kernel(x_ref, g_ref, mod_ref, *o_refs, shift_row):
    x = x_ref[...]
    y = x * lax.rsqrt(jnp.mean(x * x, axis=-1, keepdims=True) + EPS) * g_ref[...]
    h = y * (1.0 + mod_ref[shift_row + 1:shift_row + 2, :]) + mod_ref[shift_row:shift_row + 1, :]
    for o_ref in o_refs:
        o_ref[...] = h.astype(o_ref.dtype)


def norm_modulate(x, g, mod, shift_row, geom, out_dtypes):
    t, d = x.shape
    n_batch, tpb, lat = geom
    mrow = functools.partial(_mod_row, tiles_per_batch=tpb, lat_tiles=lat, n_batch=n_batch)
    outs = pl.pallas_call(
        functools.partial(_norm_mod_kernel, shift_row=shift_row),
        out_shape=[jax.ShapeDtypeStruct((t, d), dt) for dt in out_dtypes],
        grid=(t // ROW_TILE,),
        in_specs=[pl.BlockSpec((ROW_TILE, d), lambda i: (i, 0)),
                  pl.BlockSpec((1, d), lambda i: (0, 0)),
                  pl.BlockSpec((None, N_MOD, d), lambda i: (mrow(i), 0, 0))],
        out_specs=[pl.BlockSpec((ROW_TILE, d), lambda i: (i, 0)) for _ in out_dtypes],
        compiler_params=_params(("parallel",)),
        name="norm_modulate",
    )(x, g.reshape(1, d), mod)
    return outs


def _final_norm_kernel(x_ref, g_ref, o_ref):
    x = x_ref[...]
    o_ref[...] = x * lax.rsqrt(jnp.mean(x * x, axis=-1, keepdims=True) + EPS) * g_ref[...]


def final_norm(x, g):
    t, d = x.shape
    return pl.pallas_call(
        _final_norm_kernel,
        out_shape=jax.ShapeDtypeStruct((t, d), f32),
        grid=(t // ROW_TILE,),
        in_specs=[pl.BlockSpec((ROW_TILE, d), lambda i: (i, 0)),
                  pl.BlockSpec((1, d), lambda i: (0, 0))],
        out_specs=pl.BlockSpec((ROW_TILE, d), lambda i: (i, 0)),
        compiler_params=_params(("parallel",)),
        name="final_norm",
    )(x, g.reshape(1, d))


def _mla_prep_kernel(p_ref, qg_ref, kvg_ref, wq_ref, wkv_ref, cos_ref, sin_ref,
                     q_ref, k_ref, v_ref):
    def rms(v, g):
        return v * lax.rsqrt(jnp.mean(v * v, axis=-1, keepdims=True) + EPS) * g

    cos, sin = cos_ref[...], sin_ref[...]

    def rope(t):
        return t * cos + pltpu.roll(t, MLA_ROPE, axis=1) * sin

    qn = rms(p_ref[:, :Q_LORA], qg_ref[...]).astype(bf16)
    kvn = rms(p_ref[:, Q_LORA:Q_LORA + KV_LORA], kvg_ref[...]).astype(bf16)
    scale = (MLA_NOPE + MLA_ROPE) ** -0.5
    q = jnp.dot(qn, wq_ref[...], preferred_element_type=f32) * scale
    hw = 2 * LANES
    for h in range(MLA_HEADS):
        q_ref[:, h * hw:h * hw + LANES] = q[:, h * hw:h * hw + LANES].astype(bf16)
        q_ref[:, h * hw + LANES:(h + 1) * hw] = rope(q[:, h * hw + LANES:(h + 1) * hw]).astype(bf16)
    kv = jnp.dot(kvn, wkv_ref[...], preferred_element_type=f32)
    v_ref[...] = kv[:, MLA_HEADS * MLA_NOPE:].astype(bf16)
    kpe = rope(p_ref[:, Q_LORA + KV_LORA:Q_LORA + KV_LORA + LANES]).astype(bf16)
    for h in range(MLA_HEADS):
        k_ref[:, h * hw:h * hw + LANES] = kv[:, h * MLA_NOPE:(h + 1) * MLA_NOPE].astype(bf16)
        k_ref[:, h * hw + LANES:(h + 1) * hw] = kpe


def mla_prep(proj, qn_g, kvn_g, wq, wkv, layer, cos_t, sin_t, tiles_per_batch):
    t = proj.shape[0]
    pw = proj.shape[1]
    qw = MLA_HEADS * 2 * LANES
    return pl.pallas_call(
        _mla_prep_kernel,
        out_shape=[jax.ShapeDtypeStruct((t, qw), bf16),
                   jax.ShapeDtypeStruct((t, qw), bf16),
                   jax.ShapeDtypeStruct((t, MLA_HEADS * MLA_V), bf16)],
        grid=(t // ROW_TILE,),
        in_specs=[pl.BlockSpec((ROW_TILE, pw), lambda i: (i, 0)),
                  pl.BlockSpec((1, Q_LORA), lambda i: (0, 0)),
                  pl.BlockSpec((1, KV_LORA), lambda i: (0, 0)),
                  pl.BlockSpec((None,) + wq.shape[1:], lambda i: (layer, 0, 0)),
                  pl.BlockSpec((None,) + wkv.shape[1:], lambda i: (layer, 0, 0)),
                  pl.BlockSpec((ROW_TILE, LANES), lambda i: (i % tiles_per_batch, 0)),
                  pl.BlockSpec((ROW_TILE, LANES), lambda i: (i % tiles_per_batch, 0))],
        out_specs=[pl.BlockSpec((ROW_TILE, qw), lambda i: (i, 0)),
                   pl.BlockSpec((ROW_TILE, qw), lambda i: (i, 0)),
                   pl.BlockSpec((ROW_TILE, MLA_HEADS * MLA_V), lambda i: (i, 0))],
        compiler_params=_params(("parallel",)),
        name="mla_prep",
    )(proj, qn_g.reshape(1, -1), kvn_g.reshape(1, -1), wq, wkv, cos_t, sin_t)


def _mla_attn_kernel(q_ref, k_ref, v_ref, o_ref, *, lat_tiles, n_lat):
    qi = pl.program_id(2)

    def attend(k, v):
        s = lax.dot_general(q_ref[...], k, (((1,), (1,)), ((), ())), preferred_element_type=f32)
        p = jnp.exp(s - jnp.max(s, axis=-1, keepdims=True))
        l = jnp.sum(p, axis=-1, keepdims=True)
        o = jnp.dot(p.astype(bf16), v, preferred_element_type=f32)
        o_ref[...] = (o / l).astype(o_ref.dtype)

    @pl.when(qi < lat_tiles)
    def _():
        attend(k_ref[...], v_ref[...])

    @pl.when(qi >= lat_tiles)
    def _():
        attend(k_ref[n_lat:, :], v_ref[n_lat:, :])


def mla_attention(q, k, v, n_batch, lat_tiles):
    p_len = q.shape[1]
    tq = ROW_TILE
    kern = functools.partial(_mla_attn_kernel, lat_tiles=lat_tiles, n_lat=lat_tiles * ROW_TILE)
    return pl.pallas_call(
        kern,
        out_shape=jax.ShapeDtypeStruct((n_batch, p_len, MLA_HEADS * MLA_V), bf16),
        grid=(n_batch, MLA_HEADS, p_len // tq),
        in_specs=[pl.BlockSpec((None, tq, 2 * LANES), lambda b, h, i: (b, i, h)),
                  pl.BlockSpec((None, p_len, 2 * LANES), lambda b, h, i: (b, 0, h)),
                  pl.BlockSpec((None, p_len, MLA_V), lambda b, h, i: (b, 0, h))],
        out_specs=pl.BlockSpec((None, tq, MLA_V), lambda b, h, i: (b, i, h)),
        compiler_params=_params(("parallel", "parallel", "arbitrary")),
        name="mla_attention",
    )(q, k, v)


NA_QROWS = 8
NA_BAND = 16


def _na_geometry(rows):
    assert rows % NA_QROWS == 0 and rows >= NA_BAND, rows
    nblk = rows // NA_QROWS
    starts = [min(max(NA_QROWS * i - WIN_R // 2, 0), rows - NA_BAND) for i in range(nblk)]
    if nblk <= 3:
        reps, pat = list(range(nblk)), list(range(nblk))
    else:
        reps = [0, 1, nblk - 1]
        pat = [0] + [1] * (nblk - 2) + [2]
        for i in range(1, nblk - 1):
            assert starts[i] == NA_QROWS * i - WIN_R // 2
    return nblk, starts, reps, pat


def na_bias_tables(rpb, rows):
    _, starts, reps, _ = _na_geometry(rows)
    nh = rpb.shape[0]
    wr = min(WIN_R, rows)
    n_off = 2 * WIN_R - 1
    qc = jnp.arange(GRID_W)
    cs = jnp.clip(qc - WIN_C // 2, 0, GRID_W - WIN_C)
    kc = jnp.arange(GRID_W)
    col_ok = (kc[None, :] >= cs[:, None]) & (kc[None, :] < cs[:, None] + WIN_C)
    col_off = jnp.clip(kc[None, :] - qc[:, None] + (WIN_C - 1), 0, 2 * WIN_C - 2)
    toe = jnp.where(col_ok[None, None], rpb.astype(f32)[:, :, col_off], NEG_BIG)
    toe = jnp.concatenate([toe, jnp.full((nh, 1, GRID_W, GRID_W), NEG_BIG, f32)], axis=1)
    sel = []
    for i in reps:
        qr = NA_QROWS * i + jnp.arange(NA_QROWS)
        kr = starts[i] + jnp.arange(NA_BAND)
        rs = jnp.clip(qr - wr // 2, 0, rows - wr)
        row_ok = (kr[None, :] >= rs[:, None]) & (kr[None, :] < rs[:, None] + wr)
        row_off = jnp.where(row_ok, kr[None, :] - qr[:, None] + (WIN_R - 1), n_off)
        sel.append(jax.nn.one_hot(row_off, n_off + 1, dtype=f32))
    sel = jnp.stack(sel, axis=0)
    tab = jnp.einsum("pakd,hdqc->phaqkc", sel, toe, precision=lax.Precision.HIGHEST)
    return tab.reshape(len(reps), nh, NA_QROWS * GRID_W, NA_BAND * GRID_W)


def _na_kernel(q_ref, k_ref, v_ref, bias_ref, o_ref, *, n_lat, starts, pat):
    scale = NA_DIM ** -0.5
    k_ctx = k_ref[n_lat:, :]
    v_ctx = v_ref[n_lat:, :]
    nt = (((1,), (1,)), ((), ()))
    qt = NA_QROWS * GRID_W
    bt = NA_BAND * GRID_W
    for i, (st, pt) in enumerate(zip(starts, pat)):
        q = q_ref[i * qt:(i + 1) * qt, :]
        kb = k_ref[st * GRID_W:st * GRID_W + bt, :]
        vb = v_ref[st * GRID_W:st * GRID_W + bt, :]
        s_loc = lax.dot_general(q, kb, nt, preferred_element_type=f32) * scale + bias_ref[pt]
        s_ctx = lax.dot_general(q, k_ctx, nt, preferred_element_type=f32) * scale
        m = jnp.maximum(jnp.max(s_loc, axis=-1, keepdims=True), jnp.max(s_ctx, axis=-1, keepdims=True))
        p_loc = jnp.exp(s_loc - m)
        p_ctx = jnp.exp(s_ctx - m)
        l = jnp.sum(p_loc, axis=-1, keepdims=True) + jnp.sum(p_ctx, axis=-1, keepdims=True)
        o = (jnp.dot(p_loc.astype(bf16), vb, preferred_element_type=f32)
             + jnp.dot(p_ctx.astype(bf16), v_ctx, preferred_element_type=f32))
        o_ref[i * qt:(i + 1) * qt, :] = (o / l).astype(o_ref.dtype)
    qz = q_ref[n_lat:, :]
    s = lax.dot_general(qz, k_ctx, nt, preferred_element_type=f32) * scale
    p = jnp.exp(s - jnp.max(s, axis=-1, keepdims=True))
    o = jnp.dot(p.astype(bf16), v_ctx, preferred_element_type=f32) / jnp.sum(p, axis=-1, keepdims=True)
    o_ref[n_lat:, :] = o.astype(o_ref.dtype)


def neighbourhood_attention(qkv, bias, n_batch, n_lat):
    p_len = qkv.shape[1]
    rows = n_lat // GRID_W
    _, starts, _, pat = _na_geometry(rows)
    npat = bias.shape[0]
    kern = functools.partial(_na_kernel, n_lat=n_lat, starts=tuple(starts), pat=tuple(pat))
    hspec = lambda off: pl.BlockSpec((None, p_len, NA_DIM), lambda h, b: (b, 0, off + h))
    return pl.pallas_call(
        kern,
        out_shape=jax.ShapeDtypeStruct((n_batch, p_len, NA_HEADS * NA_DIM), bf16),
        grid=(NA_HEADS, n_batch),
        in_specs=[hspec(0), hspec(NA_HEADS), hspec(2 * NA_HEADS),
                  pl.BlockSpec((npat, None) + bias.shape[2:], lambda h, b: (0, h, 0, 0))],
        out_specs=pl.BlockSpec((None, p_len, NA_DIM), lambda h, b: (b, 0, h)),
        compiler_params=_params(("parallel", "parallel")),
        name="neighbourhood_attention",
    )(qkv, qkv, qkv, bias)


def _segment_conv(x, w_ref, n_lat):
    p_len = x.shape[0]
    row = lax.broadcasted_iota(jnp.int32, (p_len, 1), 0)
    local = jnp.where(row < n_lat, row, row - n_lat)
    seg_len = jnp.where(row < n_lat, n_lat, p_len - n_lat)
    y = x * w_ref[CONV_PAD_L:CONV_PAD_L + 1, :]
    for j in range(CONV_W):
        off = j - CONV_PAD_L
        if off == 0:
            continue
        shifted = pltpu.roll(x, (-off) % p_len, axis=0)
        ok = (local + off >= 0) & (local + off < seg_len)
        y = y + jnp.where(ok, shifted, 0.0) * w_ref[j:j + 1, :]
    return y


SCAN_CHUNK = 256


def _chunk_scan(a, b, reverse):
    n = a.shape[0]
    row = lax.broadcasted_iota(jnp.int32, (n, 1), 0)
    k = 1
    while k < n:
        if reverse:
            a_s = pltpu.roll(a, n - k, axis=0)
            b_s = pltpu.roll(b, n - k, axis=0)
            ok = row < n - k
        else:
            a_s = pltpu.roll(a, k, axis=0)
            b_s = pltpu.roll(b, k, axis=0)
            ok = row >= k
        b = b + a * jnp.where(ok, b_s, 0.0)
        a = a * jnp.where(ok, a_s, 1.0)
        k *= 2
    return a, b


def _lru_kernel(lu_ref, ly_ref, cw_ref, cb_ref, wa_ref, ba_ref, wx_ref, bx_ref, lam_ref, o_ref,
                u_scr, a_scr, b_scr, h_scr, *, n_lat):
    p_len = lu_ref.shape[0]
    u_scr[...] = _segment_conv(lu_ref[...], cw_ref, n_lat) + cb_ref[...]
    segments = ((n_lat, p_len - n_lat), (0, n_lat))

    for d, reverse in enumerate((False, True)):
        ub = u_scr[...].astype(bf16)
        r = _sigmoid(jnp.dot(ub, wa_ref[d].astype(bf16), preferred_element_type=f32) + ba_ref[d:d + 1, :])
        g = _sigmoid(jnp.dot(ub, wx_ref[d].astype(bf16), preferred_element_type=f32) + bx_ref[d:d + 1, :])
        log_a = (-LRU_C * r) * _softplus(-lam_ref[d:d + 1, :])
        a_scr[...] = jnp.exp(log_a)
        th = jnp.tanh(log_a)
        b_scr[...] = jnp.sqrt(-2.0 * th / (1.0 - th)) * (g * u_scr[...])

        state = jnp.zeros((1, LANES), f32)
        for start, length in segments:
            n_chunks = length // SCAN_CHUNK

            def body(c, h_prev, start=start, n_chunks=n_chunks, reverse=reverse):
                ci = (n_chunks - 1 - c) if reverse else c
                rows = pl.ds(pl.multiple_of(start + ci * SCAN_CHUNK, SCAN_CHUNK), SCAN_CHUNK)
                a_cum, h = _chunk_scan(a_scr[rows, :], b_scr[rows, :], reverse)
                h = h + a_cum * h_prev
                if d == 0:
                    h_scr[rows, :] = h
                else:
                    h_scr[rows, :] = h_scr[rows, :] + h
                return h[0:1, :] if reverse else h[SCAN_CHUNK - 1:SCAN_CHUNK, :]

            state = lax.fori_loop(0, n_chunks, body, state)

    o_ref[...] = (h_scr[...] * _gelu_tanh(ly_ref[...])).astype(o_ref.dtype)


def rglru(luy, conv_w, conv_b, wa, ba, wx, bx, lam, n_batch, n_lat):
    p_len = luy.shape[1]
    nblk = LRU_W // LANES
    assert LRU_W // LRU_BLOCKS == LANES and n_lat % SCAN_CHUNK == 0 and (p_len - n_lat) % SCAN_CHUNK == 0
    vec = lambda rows: pl.BlockSpec((rows, LANES), lambda b, j: (0, j))
    return pl.pallas_call(
        functools.partial(_lru_kernel, n_lat=n_lat),
        out_shape=jax.ShapeDtypeStruct((n_batch, p_len, LRU_W), bf16),
        grid=(n_batch, nblk),
        in_specs=[pl.BlockSpec((None, p_len, LANES), lambda b, j: (b, 0, j)),
                  pl.BlockSpec((None, p_len, LANES), lambda b, j: (b, 0, nblk + j)),
                  vec(CONV_W), vec(1),
                  pl.BlockSpec((2, None, LANES, LANES), lambda b, j: (0, j, 0, 0)), vec(2),
                  pl.BlockSpec((2, None, LANES, LANES), lambda b, j: (0, j, 0, 0)), vec(2),
                  vec(2)],
        out_specs=pl.BlockSpec((None, p_len, LANES), lambda b, j: (b, 0, j)),
        scratch_shapes=[pltpu.VMEM((p_len, LANES), f32) for _ in range(4)],
        compiler_params=_params(("parallel", "parallel")),
        name="rglru",
    )(luy, luy, conv_w, conv_b.reshape(1, LRU_W), wa, ba, wx, bx, lam)


def _dn_pre_kernel(x_ref, w_ref, o_ref, *, n_lat):
    j = pl.program_id(1)
    u = _silu(_segment_conv(x_ref[...], w_ref, n_lat))
    nrm = u * lax.rsqrt(jnp.sum(u * u, axis=-1, keepdims=True) + EPS)
    qk_scale = jnp.where(j < DN_HEADS, DN_DK ** -0.5, 1.0)
    o_ref[...] = jnp.where(j < 2 * DN_HEADS, nrm * qk_scale, u).astype(o_ref.dtype)


def deltanet_pre(dqkv, conv_w, n_batch, n_lat, col_blocks):
    p_len = dqkv.shape[1]
    return pl.pallas_call(
        functools.partial(_dn_pre_kernel, n_lat=n_lat),
        out_shape=jax.ShapeDtypeStruct((n_batch, p_len, col_blocks * LANES), bf16),
        grid=(n_batch, col_blocks),
        in_specs=[pl.BlockSpec((None, p_len, LANES), lambda b, j: (b, 0, j)),
                  pl.BlockSpec((CONV_W, LANES), lambda b, j: (0, j))],
        out_specs=pl.BlockSpec((None, p_len, LANES), lambda b, j: (b, 0, j)),
        compiler_params=_params(("parallel", "parallel")),
        name="deltanet_pre",
    )(dqkv, conv_w)


def _dn_gate_kernel(ab_ref, alog_ref, dt_ref, o_ref):
    nh2 = 2 * DN_HEADS
    a_raw = ab_ref[:, :nh2]
    b_raw = ab_ref[:, nh2:2 * nh2]
    g = -jnp.exp(alog_ref[...]) * _softplus(a_raw + dt_ref[...])
    beta = _sigmoid(b_raw)
    n = g.shape[0]
    r = lax.broadcasted_iota(jnp.int32, (n, n), 0)
    c = lax.broadcasted_iota(jnp.int32, (n, n), 1)
    same = (r // DN_CHUNK) == (c // DN_CHUNK)
    hi = lax.Precision.HIGHEST
    pre = jnp.dot(jnp.where(same & (c <= r), 1.0, 0.0), g, preferred_element_type=f32, precision=hi)
    suf = jnp.dot(jnp.where(same & (c >= r), 1.0, 0.0), g, preferred_element_type=f32, precision=hi)
    tot = jnp.dot(jnp.where(same, 1.0, 0.0), g, preferred_element_type=f32, precision=hi)
    pad = jnp.zeros((n, LANES - 3 * nh2), f32)
    o_ref[...] = jnp.concatenate([pre[:, :DN_HEADS], suf[:, DN_HEADS:], beta, tot, pad], axis=-1)


def deltanet_gates(ab, col_block, a_log, dt_bias):
    t = ab.shape[0]
    nh2 = 2 * DN_HEADS
    return pl.pallas_call(
        _dn_gate_kernel,
        out_shape=jax.ShapeDtypeStruct((t, LANES), f32),
        grid=(t // ROW_TILE,),
        in_specs=[pl.BlockSpec((ROW_TILE, LANES), lambda i: (i, col_block)),
                  pl.BlockSpec((1, nh2), lambda i: (0, 0)),
                  pl.BlockSpec((1, nh2), lambda i: (0, 0))],
        out_specs=pl.BlockSpec((ROW_TILE, LANES), lambda i: (i, 0)),
        compiler_params=_params(("parallel",)),
        name="deltanet_gates",
    )(ab, a_log.reshape(1, nh2), dt_bias.reshape(1, nh2))


def _tri_inverse_minus_eye(low):
    c = low.shape[-1]
    bmm = functools.partial(jnp.einsum, "gij,gjk->gik", preferred_element_type=f32)
    x = -low
    acc = x
    k = 2
    while k < c:
        xb = x.astype(bf16)
        x = bmm(xb, xb)
        acc = acc + x + bmm(x.astype(bf16), acc.astype(bf16))
        k *= 2
    return acc


DN_GROUP = 4


def _dn_kernel(q_ref, k_ref, v_ref, z_ref, gcol_ref, grow_ref, ng_ref, o_ref,
               wkq_scr, wv_scr, kd_scr, ain_scr, gend_scr, acc_scr, *, n_lat):
    p_len = q_ref.shape[0]
    cs = DN_CHUNK
    n_chunks = p_len // cs
    nc_lat = n_lat // cs
    nc_ctx = n_chunks - nc_lat
    tn = (((0,), (0,)), ((), ()))
    grp = DN_GROUP
    nb = 2 * grp
    ri = lax.broadcasted_iota(jnp.int32, (nb, cs, cs), 1)
    ci = lax.broadcasted_iota(jnp.int32, (nb, cs, cs), 2)
    rev = lax.broadcasted_iota(jnp.int32, (nb, cs, cs), 0) >= grp
    ahead = jnp.where(rev, ri - ci, ci - ri)
    incl = ahead <= 0
    strict = ahead < 0

    def phase1(g, carry):
        c0 = g * grp
        rows = pl.ds(pl.multiple_of(c0 * cs, cs * grp), cs * grp)

        def both(x):
            x = x.astype(f32).reshape(grp, cs, x.shape[-1])
            return jnp.concatenate([x, x], axis=0)

        def per_dir(col):
            return jnp.concatenate([gcol_ref[rows, col + d:col + d + 1].reshape(grp, cs, 1) for d in range(2)], axis=0)

        q, k, v = both(q_ref[rows, :]), both(k_ref[rows, :]), both(v_ref[rows, :])
        gc, beta, gtot = per_dir(0), per_dir(2), per_dir(4)
        gr = jnp.concatenate([grow_ref[d, pl.ds(c0, grp), :] for d in range(2)], axis=0)[:, None, :]
        decay = jnp.where(incl, jnp.exp(jnp.where(incl, gc - gr, 0.0)), 0.0)
        e_gc = jnp.exp(gc)
        kb = k * beta
        gram = jnp.einsum("gik,gjk->gij", jnp.concatenate([kb, q], axis=1).astype(bf16), k.astype(bf16),
                          preferred_element_type=f32)
        t_m1 = _tri_inverse_minus_eye(jnp.where(strict, gram[:, :cs] * decay, 0.0))
        rhs = jnp.concatenate([v * beta, kb * e_gc], axis=2)
        w = rhs + jnp.einsum("gij,gjk->gik", t_m1.astype(bf16), rhs.astype(bf16), preferred_element_type=f32)
        q_dec = (q * e_gc).astype(bf16)
        k_dec = (k * jnp.exp(gtot - gc)).astype(bf16)
        a_in = (gram[:, cs:] * decay).astype(bf16)
        g_end = jnp.broadcast_to(jnp.exp(gtot[:, 0:1, :]), (nb, 1, DN_DV))
        for d in range(2):
            sl = slice(d * grp, (d + 1) * grp)
            dst = pl.ds(c0, grp)
            wv_scr[d, dst] = w[sl, :, :DN_DV]
            wkq_scr[d, dst, :cs, :] = w[sl, :, DN_DV:].astype(bf16)
            wkq_scr[d, dst, cs:, :] = q_dec[sl]
            kd_scr[d, dst] = k_dec[sl]
            ain_scr[d, dst] = a_in[sl]
            gend_scr[d, dst] = g_end[sl]
        return carry

    lax.fori_loop(0, n_chunks // grp, phase1, 0)
    acc_scr[...] = jnp.zeros_like(acc_scr)

    def phase2(t, states):
        in_ctx = t < nc_ctx
        chunk = (jnp.where(in_ctx, nc_lat + t, t - nc_ctx),
                 jnp.where(in_ctx, n_chunks - 1 - t, nc_lat - 1 - (t - nc_ctx)))
        new_states = []
        for d in range(2):
            c = chunk[d]
            s = states[d]
            a = jnp.dot(wkq_scr[d, c], s.astype(bf16), preferred_element_type=f32)
            v_new = (wv_scr[d, c] - a[:cs]).astype(bf16)
            o = a[cs:] + jnp.dot(ain_scr[d, c], v_new, preferred_element_type=f32)
            rows = pl.ds(pl.multiple_of(c * cs, cs), cs)
            acc_scr[rows, :] = acc_scr[rows, :] + o
            new_states.append(s * gend_scr[d, c] + lax.dot_general(kd_scr[d, c], v_new, tn,
                                                                    preferred_element_type=f32))
        return tuple(new_states)

    zero = jnp.zeros((DN_DK, DN_DV), f32)
    lax.fori_loop(0, n_chunks, phase2, (zero, zero))

    o = acc_scr[...]
    y = o * lax.rsqrt(jnp.mean(o * o, axis=-1, keepdims=True) + EPS) * ng_ref[...]
    o_ref[...] = (y * _silu(z_ref[...])).astype(o_ref.dtype)


def deltanet(qkv, dz_src, z_col0, gcol, grow, norm_g, n_batch, n_lat):
    p_len = qkv.shape[1]
    n_chunks = p_len // DN_CHUNK
    assert n_chunks % DN_GROUP == 0 and n_lat % DN_CHUNK == 0
    hspec = lambda off: pl.BlockSpec((None, p_len, LANES), lambda b, h: (b, 0, off + h))
    return pl.pallas_call(
        functools.partial(_dn_kernel, n_lat=n_lat),
        out_shape=jax.ShapeDtypeStruct((n_batch, p_len, DN_HEADS * DN_DV), bf16),
        grid=(n_batch, DN_HEADS),
        in_specs=[hspec(0), hspec(DN_HEADS), hspec(2 * DN_HEADS),
                  pl.BlockSpec((None, p_len, LANES), lambda b, h: (b, 0, z_col0 + h)),
                  pl.BlockSpec((None, None, p_len, 8), lambda b, h: (b, h, 0, 0)),
                  pl.BlockSpec((None, None, 2, p_len // DN_CHUNK, DN_CHUNK), lambda b, h: (b, h, 0, 0, 0)),
                  pl.BlockSpec((1, DN_DV), lambda b, h: (0, 0))],
        out_specs=pl.BlockSpec((None, p_len, LANES), lambda b, h: (b, 0, h)),
        scratch_shapes=[pltpu.VMEM((2, n_chunks, 2 * DN_CHUNK, DN_DK), bf16),
                        pltpu.VMEM((2, n_chunks, DN_CHUNK, DN_DV), f32),
                        pltpu.VMEM((2, n_chunks, DN_CHUNK, DN_DK), bf16),
                        pltpu.VMEM((2, n_chunks, DN_CHUNK, DN_CHUNK), bf16),
                        pltpu.VMEM((2, n_chunks, 1, DN_DV), f32),
                        pltpu.VMEM((p_len, DN_DV), f32)],
        compiler_params=_params(("parallel", "parallel")),
        name="deltanet",
    )(qkv, qkv, qkv, dz_src, gcol, grow, norm_g.reshape(1, DN_DV))


def _merge_kernel(*refs):
    o_refs, g_refs, wb_ref, out_ref = refs[:N_BRANCH], refs[N_BRANCH:2 * N_BRANCH], refs[-2], refs[-1]
    acc = None
    for i in range(N_BRANCH):
        y = jnp.dot(o_refs[i][...], wb_ref[i], preferred_element_type=f32)
        y = _sigmoid(g_refs[i][...].astype(f32)) * y
        acc = y if acc is None else acc + y
    out_ref[...] = acc.astype(out_ref.dtype)


def merge_branches(branches, gates, w_branch, layer):
    t, bw = branches[0].shape
    d = w_branch.shape[3]
    tm, tn = ROW_TILE * 2, min(1024, d)
    while t % tm:
        tm //= 2
    nj = d // tn
    bspec = pl.BlockSpec((tm, bw), lambda j, i: (i, 0))
    gspecs = [pl.BlockSpec((tm, tn), functools.partial(lambda j, i, br: (i, br * nj + j), br=br))
              for br in range(N_BRANCH)]
    return pl.pallas_call(
        _merge_kernel,
        out_shape=jax.ShapeDtypeStruct((t, d), bf16),
        grid=(nj, t // tm),
        in_specs=[bspec] * N_BRANCH + gspecs + [pl.BlockSpec((None, N_BRANCH, bw, tn),
                                                             lambda j, i: (layer, 0, 0, j))],
        out_specs=pl.BlockSpec((tm, tn), lambda j, i: (i, j)),
        compiler_params=_params(("parallel", "parallel")),
        name="merge_branches",
    )(*branches, *([gates] * N_BRANCH), w_branch)


def _out_proj_kernel(m_ref, w_ref, x_ref, mod_ref, o_ref, *, gate_row):
    y = jnp.dot(m_ref[...], w_ref[...], preferred_element_type=f32)
    o_ref[...] = x_ref[...] + mod_ref[gate_row:gate_row + 1, :] * y


def out_proj_residual(merged, w_out, layer, x, mod, gate_row, geom):
    t, d = x.shape
    n_batch, tpb, lat = geom
    mrow = functools.partial(_mod_row, tiles_per_batch=tpb, lat_tiles=lat, n_batch=n_batch)
    return pl.pallas_call(
        functools.partial(_out_proj_kernel, gate_row=gate_row),
        out_shape=jax.ShapeDtypeStruct((t, d), f32),
        grid=(t // ROW_TILE,),
        in_specs=[pl.BlockSpec((ROW_TILE, d), lambda i: (i, 0)),
                  pl.BlockSpec((None, d, d), lambda i: (layer, 0, 0)),
                  pl.BlockSpec((ROW_TILE, d), lambda i: (i, 0)),
                  pl.BlockSpec((None, N_MOD, d), lambda i: (mrow(i), 0, 0))],
        out_specs=pl.BlockSpec((ROW_TILE, d), lambda i: (i, 0)),
        compiler_params=_params(("parallel",)),
        name="out_proj_residual",
    )(merged, w_out, x, mod)


def _route_kernel(h_ref, rw_ref, rb_ref, sel_ref, cnt_ref, carry):
    i = pl.program_id(0)

    @pl.when(i == 0)
    def _():
        carry[...] = jnp.zeros_like(carry)

    logits = jnp.dot(h_ref[...], rw_ref[...], preferred_element_type=f32,
                     precision=lax.Precision.HIGHEST) + rb_ref[...]
    n = logits.shape[0]
    lane = lax.broadcasted_iota(jnp.int32, (n, LANES), 1).astype(f32)
    work = logits
    vals, idxs, hots = [], [], []
    for _ in range(TOP_K):
        m = jnp.max(work, axis=-1, keepdims=True)
        idx = jnp.min(jnp.where(work == m, lane, float(LANES)), axis=-1, keepdims=True)
        hot = lane == idx
        vals.append(m)
        idxs.append(idx)
        hots.append(hot)
        work = jnp.where(hot, NEG_BIG * 2.0, work)
    exps = [jnp.exp(v - vals[0]) for v in vals]
    den = exps[0]
    for e in exps[1:]:
        den = den + e
    onehot = jnp.zeros((n, LANES), f32)
    for hot in hots:
        onehot = onehot + jnp.where(hot, 1.0, 0.0)
    r = lax.broadcasted_iota(jnp.int32, (n, n), 0)
    c = lax.broadcasted_iota(jnp.int32, (n, n), 1)
    before = jnp.dot(jnp.where(c < r, 1.0, 0.0).astype(bf16), onehot.astype(bf16),
                     preferred_element_type=f32) + carry[...]
    out = jnp.zeros((n, LANES), f32)
    for kk in range(TOP_K):
        rank = jnp.sum(jnp.where(hots[kk], before, 0.0), axis=-1, keepdims=True)
        out = jnp.where(lane == float(kk), idxs[kk], out)
        out = jnp.where(lane == float(TOP_K + kk), rank, out)
        out = jnp.where(lane == float(2 * TOP_K + kk), exps[kk] / den, out)
    sel_ref[...] = out
    carry[...] = carry[...] + jnp.sum(onehot, axis=0, keepdims=True)
    cnt_ref[...] = carry[...]


def moe_route(h, router_w, router_b):
    t, d = h.shape
    e = router_w.shape[1]
    rw = jnp.zeros((d, LANES), f32).at[:, :e].set(router_w)
    rb = jnp.full((1, LANES), NEG_BIG, f32).at[0, :e].set(router_b)
    return pl.pallas_call(
        _route_kernel,
        out_shape=[jax.ShapeDtypeStruct((t, LANES), f32), jax.ShapeDtypeStruct((1, LANES), f32)],
        grid=(t // ROW_TILE,),
        in_specs=[pl.BlockSpec((ROW_TILE, d), lambda i: (i, 0)),
                  pl.BlockSpec((d, LANES), lambda i: (0, 0)),
                  pl.BlockSpec((1, LANES), lambda i: (0, 0))],
        out_specs=[pl.BlockSpec((ROW_TILE, LANES), lambda i: (i, 0)),
                   pl.BlockSpec((1, LANES), lambda i: (0, 0))],
        scratch_shapes=[pltpu.VMEM((1, LANES), f32)],
        compiler_params=_params(("arbitrary",)),
        name="moe_route",
    )(h, rw, rb)


def _dispatch_kernel(pos_hbm, x_ref, init_ref, xs_hbm, pos_smem, sem_pos, sem_rows):
    del init_ref
    i = pl.program_id(0)
    n_sel = ROW_TILE * TOP_K
    cp = pltpu.make_async_copy(pos_hbm.at[i], pos_smem, sem_pos)
    cp.start()
    cp.wait()

    def issue(r, c):
        src = x_ref.at[pl.ds(r, 1)]
        for kk in range(TOP_K):
            pltpu.make_async_copy(src, xs_hbm.at[pl.ds(pos_smem[r * TOP_K + kk], 1)], sem_rows).start()
        return c

    lax.fori_loop(0, ROW_TILE, issue, 0, unroll=4)
    pltpu.make_async_copy(xs_hbm.at[pl.ds(0, n_sel)], xs_hbm.at[pl.ds(0, n_sel)], sem_rows).wait()


def moe_dispatch(h, pos, n_slots):
    t, d = h.shape
    tiles = t // ROW_TILE
    init = jnp.zeros((n_slots, d), f32)
    return pl.pallas_call(
        _dispatch_kernel,
        out_shape=jax.ShapeDtypeStruct((n_slots, d), f32),
        grid=(tiles,),
        in_specs=[pl.BlockSpec(memory_space=pl.ANY),
                  pl.BlockSpec((ROW_TILE, d), lambda i: (i, 0)),
                  pl.BlockSpec(memory_space=pl.ANY)],
        out_specs=pl.BlockSpec(memory_space=pl.ANY),
        scratch_shapes=[pltpu.SMEM((ROW_TILE * TOP_K,), jnp.int32),
                        pltpu.SemaphoreType.DMA, pltpu.SemaphoreType.DMA],
        input_output_aliases={2: 0},
        compiler_params=_params(("arbitrary",)),
        name="moe_dispatch",
    )(pos.reshape(tiles, ROW_TILE * TOP_K), h, init)


def _expert_kernel(te_ref, xs_ref, w1_ref, b1_ref, sel_ref, w2_ref, b2_ref, ys_ref):
    del te_ref
    u = jnp.dot(xs_ref[...].astype(bf16), w1_ref[...], preferred_element_type=f32) + b1_ref[...]
    n = u.shape[1]
    lin = pltpu.roll(u, n - 1, axis=1)
    glu = jnp.minimum(u, SWIGLU_LIMIT)
    lin = jnp.clip(lin, -SWIGLU_LIMIT, SWIGLU_LIMIT)
    act = glu * _sigmoid(SWIGLU_ALPHA * glu) * (lin + 1.0)
    even = lax.broadcasted_iota(jnp.int32, u.shape, 1) % 2 == 0
    act = jnp.where(even, act, 0.0).astype(bf16)
    act = jnp.dot(act, sel_ref[...], preferred_element_type=f32).astype(bf16)
    ys_ref[...] = jnp.dot(act, w2_ref[...], preferred_element_type=f32) + b2_ref[...]


def moe_experts(xs, tile_expert, layer, w1, b1, w2, b2):
    s, d = xs.shape
    ff2 = w1.shape[3]
    ff = ff2 // 2
    sel = (jnp.arange(ff2)[:, None] == 2 * jnp.arange(ff)[None, :]).astype(bf16)
    grid_spec = pltpu.PrefetchScalarGridSpec(
        num_scalar_prefetch=1,
        grid=(s // ROW_TILE,),
        in_specs=[pl.BlockSpec((ROW_TILE, d), lambda i, te: (i, 0)),
                  pl.BlockSpec((None, None, d, ff2), lambda i, te: (layer, te[i], 0, 0)),
                  pl.BlockSpec((None, None, 1, ff2), lambda i, te: (layer, te[i], 0, 0)),
                  pl.BlockSpec((ff2, ff), lambda i, te: (0, 0)),
                  pl.BlockSpec((None, None, ff, d), lambda i, te: (layer, te[i], 0, 0)),
                  pl.BlockSpec((None, None, 1, d), lambda i, te: (layer, te[i], 0, 0))],
        out_specs=pl.BlockSpec((ROW_TILE, d), lambda i, te: (i, 0)),
    )
    return pl.pallas_call(
        _expert_kernel,
        out_shape=jax.ShapeDtypeStruct((s, d), f32),
        grid_spec=grid_spec,
        compiler_params=_params(("arbitrary",)),
        name="moe_experts",
    )(tile_expert, xs, w1, b1, sel, w2, b2)


def _combine_kernel(pos_hbm, ys_hbm, x_ref, wt_ref, mod_ref, o_ref, buf, pos_smem, sem_pos, sem_rows, *, gate_row):
    i = pl.program_id(0)
    n_sel = ROW_TILE * TOP_K
    cp = pltpu.make_async_copy(pos_hbm.at[i], pos_smem, sem_pos)
    cp.start()
    cp.wait()

    def issue(r, c):
        for kk in range(TOP_K):
            pltpu.make_async_copy(ys_hbm.at[pl.ds(pos_smem[r * TOP_K + kk], 1)],
                                  buf.at[kk, pl.ds(r, 1)], sem_rows).start()
        return c

    lax.fori_loop(0, ROW_TILE, issue, 0, unroll=4)
    pltpu.make_async_copy(ys_hbm.at[pl.ds(0, n_sel)], ys_hbm.at[pl.ds(0, n_sel)], sem_rows).wait()
    acc = wt_ref[:, 2 * TOP_K:2 * TOP_K + 1] * buf[0]
    for kk in range(1, TOP_K):
        acc = acc + wt_ref[:, 2 * TOP_K + kk:2 * TOP_K + kk + 1] * buf[kk]
    o_ref[...] = x_ref[...] + mod_ref[gate_row:gate_row + 1, :] * acc


def moe_combine(ys, pos, sel, x, mod, gate_row, geom):
    t, d = x.shape
    tiles = t // ROW_TILE
    n_batch, tpb, lat = geom
    mrow = functools.partial(_mod_row, tiles_per_batch=tpb, lat_tiles=lat, n_batch=n_batch)
    return pl.pallas_call(
        functools.partial(_combine_kernel, gate_row=gate_row),
        out_shape=jax.ShapeDtypeStruct((t, d), f32),
        grid=(tiles,),
        in_specs=[pl.BlockSpec(memory_space=pl.ANY),
                  pl.BlockSpec(memory_space=pl.ANY),
                  pl.BlockSpec((ROW_TILE, d), lambda i: (i, 0)),
                  pl.BlockSpec((ROW_TILE, LANES), lambda i: (i, 0)),
                  pl.BlockSpec((None, N_MOD, d), lambda i: (mrow(i), 0, 0))],
        out_specs=pl.BlockSpec((ROW_TILE, d), lambda i: (i, 0)),
        scratch_shapes=[pltpu.VMEM((TOP_K, ROW_TILE, d), f32),
                        pltpu.SMEM((ROW_TILE * TOP_K,), jnp.int32),
                        pltpu.SemaphoreType.DMA, pltpu.SemaphoreType.DMA],
        compiler_params=_params(("arbitrary",)),
        name="moe_combine",
    )(pos.reshape(tiles, ROW_TILE * TOP_K), ys, x, sel, mod)


def moe_ffn_residual(h, x, mod, gate_row, geom, router_w, router_b, layer, w1, b1, w2, b2):
    t, d = h.shape
    n_exp = router_w.shape[1]
    sel, counts = moe_route(h, router_w, router_b)
    eidx = sel[:, :TOP_K].astype(jnp.int32)
    rank = sel[:, TOP_K:2 * TOP_K].astype(jnp.int32)
    counts = counts[0, :n_exp].astype(jnp.int32)
    padded = ((counts + ROW_TILE - 1) // ROW_TILE) * ROW_TILE
    ends = jnp.cumsum(padded)
    starts = ends - padded
    pos = starts[eidx] + rank
    n_tiles = (t * TOP_K) // ROW_TILE + n_exp
    tile_start = jnp.arange(n_tiles, dtype=jnp.int32) * ROW_TILE
    tile_expert = jnp.minimum(jnp.sum((ends[None, :] <= tile_start[:, None]).astype(jnp.int32), axis=1), n_exp - 1)
    xs = moe_dispatch(h, pos, n_tiles * ROW_TILE)
    ys = moe_experts(xs, tile_expert, layer, w1, b1, w2, b2)
    return moe_combine(ys, pos, sel, x, mod, gate_row, geom)


def _rope_perm(w):
    lead = w.shape[:-1]
    nf = MLA_ROPE // 4
    wr = w.reshape(*lead, 2, 2, nf)
    return jnp.stack([-wr[..., 1, :], wr[..., 0, :]], axis=-2).reshape(*lead, MLA_ROPE)


NA_COLS = 3 * NA_HEADS * NA_DIM
LRU_COLS = 2 * LRU_W
DN_COLS = DN_HEADS * (2 * DN_DK + DN_DV) + DN_HEADS * DN_DV
MLA_COLS = Q_LORA + KV_LORA + 2 * LANES


def regroup_w_in(w_in, d_model):
    sizes = (Q_LORA, KV_LORA, MLA_ROPE, NA_HEADS * NA_DIM, NA_HEADS * NA_DIM, NA_HEADS * NA_DIM, LRU_W, LRU_W,
             DN_HEADS * (2 * DN_DK + DN_DV), DN_HEADS * DN_DV, 2 * DN_HEADS, 2 * DN_HEADS, N_BRANCH * d_model)
    parts, s = [], 0
    for n in sizes:
        parts.append(w_in[:, :, s:s + n])
        s += n
    assert s == w_in.shape[2]
    (w_qc, w_kvc, w_kpe, w_naq, w_nak, w_nav, w_lu, w_ly, w_dqkv, w_dz, w_da, w_db, w_gt) = parts
    pad = jnp.zeros(w_in.shape[:2] + (LANES - 4 * DN_HEADS,), w_in.dtype)
    w = jnp.concatenate([w_naq, w_nak, w_nav, w_lu, w_ly, w_dqkv, w_dz, w_gt,
                         w_qc, w_kvc, w_kpe, _rope_perm(w_kpe), w_da, w_db, pad], axis=2).astype(bf16)
    off_na = 0
    off_lru = off_na + NA_COLS
    off_dn = off_lru + LRU_COLS
    off_gate = off_dn + DN_COLS
    off_mla = off_gate + N_BRANCH * d_model
    assert w.shape[2] == off_mla + MLA_COLS
    return w, (off_na, off_lru, off_dn, off_gate, off_mla)


def rope_tables(n_lat, p_len):
    t = jnp.arange(n_lat)
    row = (t // GRID_W).astype(f32)
    col = (t % GRID_W).astype(f32)
    nf = MLA_ROPE // 4
    inv_freq = ROPE_BASE ** (-jnp.arange(nf, dtype=f32) / nf)
    ang = jnp.concatenate([row[:, None] * inv_freq] * 2 + [col[:, None] * inv_freq] * 2, axis=1)
    zeros = jnp.zeros((n_lat, LANES - MLA_ROPE), f32)
    cos_t = jnp.concatenate([jnp.cos(ang), zeros], axis=1)
    sin_t = jnp.concatenate([jnp.sin(ang), zeros], axis=1)
    n_ctx = p_len - n_lat
    ctx_cos = jnp.concatenate([jnp.ones((n_ctx, MLA_ROPE), f32), jnp.zeros((n_ctx, LANES - MLA_ROPE), f32)], axis=1)
    return (jnp.concatenate([cos_t, ctx_cos], axis=0),
            jnp.concatenate([sin_t, jnp.zeros((n_ctx, LANES), f32)], axis=0))


def prepare_weights(w_in, mla_wq_up, mla_wkv_up, w_branch, w_out, exp_w1, exp_b1, exp_w2, exp_b2, d_model):
    depth = w_in.shape[0]
    w_all, offs = regroup_w_in(w_in, d_model)
    wq = mla_wq_up.reshape(depth, Q_LORA, MLA_HEADS, MLA_NOPE + MLA_ROPE)
    wq = jnp.concatenate([wq, _rope_perm(wq[..., MLA_NOPE:])], axis=-1).reshape(depth, Q_LORA, -1).astype(bf16)
    wkv = mla_wkv_up.reshape(depth, KV_LORA, MLA_HEADS, MLA_NOPE + MLA_V)
    wkv = jnp.concatenate([wkv[..., :MLA_NOPE].reshape(depth, KV_LORA, -1),
                           wkv[..., MLA_NOPE:].reshape(depth, KV_LORA, -1)], axis=2).astype(bf16)
    return dict(w_all=w_all, offs=offs, wq=wq, wkv=wkv, w_branch=w_branch.astype(bf16), w_out=w_out.astype(bf16),
                exp_w1=exp_w1.astype(bf16), exp_b1=exp_b1[:, :, None, :], exp_w2=exp_w2.astype(bf16),
                exp_b2=exp_b2[:, :, None, :])


def token_mixer_branches(h, l, wts, geom, n_lat, cos_t, sin_t, na_bias, mla_qn_g, mla_kvn_g, lru_conv_w, lru_conv_b,
                         lru_wa, lru_ba, lru_wx, lru_bx, lru_lam, dn_conv_w, dn_a_log, dn_dt_bias, dn_norm_g):
    n_batch, tpb, lat_tiles = geom
    t, d = h.shape
    p_len = t // n_batch
    w_all = wts["w_all"]
    off_na, off_lru, off_dn, off_gate, off_mla = wts["offs"]

    p_mla = matmul(h, w_all, l, off_mla, MLA_COLS, f32, 1024, 256, "proj_mla")
    q, k, v = mla_prep(p_mla, mla_qn_g, mla_kvn_g, wts["wq"], wts["wkv"], l, cos_t, sin_t, tpb)
    o_a = mla_attention(q.reshape(n_batch, p_len, -1), k.reshape(n_batch, p_len, -1),
                        v.reshape(n_batch, p_len, -1), n_batch, lat_tiles)

    p_na = matmul(h, w_all, l, off_na, NA_COLS, bf16, 1024, 512, "proj_na")
    o_b = neighbourhood_attention(p_na.reshape(n_batch, p_len, -1), na_bias, n_batch, n_lat)

    p_lru = matmul(h, w_all, l, off_lru, LRU_COLS, f32, 1024, 512, "proj_lru")
    o_c = rglru(p_lru.reshape(n_batch, p_len, -1), lru_conv_w, lru_conv_b, lru_wa, lru_ba, lru_wx, lru_bx, lru_lam,
                n_batch, n_lat)

    p_dn = matmul(h, w_all, l, off_dn, DN_COLS, f32, 1024, 512, "proj_dn").reshape(n_batch, p_len, -1)
    qkv_dn = deltanet_pre(p_dn, dn_conv_w, n_batch, n_lat, 3 * DN_HEADS)
    gates = deltanet_gates(p_mla, (MLA_COLS - LANES) // LANES, dn_a_log, dn_dt_bias).reshape(n_batch, p_len, LANES)
    nh = DN_HEADS

    def per_head(cols):
        return jnp.transpose(cols.reshape(n_batch, p_len, 2, nh), (0, 3, 1, 2))

    gc_cols = per_head(gates[..., :2 * nh])
    gcol = jnp.concatenate([gc_cols, per_head(gates[..., 2 * nh:4 * nh]), per_head(gates[..., 4 * nh:6 * nh]),
                            jnp.zeros((n_batch, nh, p_len, 2), f32)], axis=-1)
    grow = jnp.transpose(gc_cols, (0, 1, 3, 2)).reshape(n_batch, nh, 2, p_len // DN_CHUNK, DN_CHUNK)
    o_d = deltanet(qkv_dn, p_dn, 3 * DN_HEADS, gcol, grow, dn_norm_g, n_batch, n_lat)

    gate_logits = matmul(h, w_all, l, off_gate, N_BRANCH * d, bf16, 1024, 512, "proj_gate")
    return o_a, o_b, o_c, o_d, gate_logits


def kernel(x, c, ctx, c_ctx, ada_w, ada_b, norm_mix_g, norm_ffn_g, w_in, mla_qn_g, mla_wq_up, mla_kvn_g, mla_wkv_up, na_rpb, lru_conv_w, lru_conv_b, lru_wa, lru_ba, lru_wx, lru_bx, lru_lam, dn_conv_w, dn_a_log, dn_dt_bias, dn_norm_g, w_branch, w_out, router_w, router_b, exp_w1, exp_b1, exp_w2, exp_b2, final_g):
    n_batch, n_lat, d = x.shape
    n_ctx = ctx.shape[1]
    depth = ada_w.shape[0]
    p_len = n_lat + n_ctx
    t = n_batch * p_len
    assert n_lat % ROW_TILE == 0 and n_ctx % ROW_TILE == 0 and n_lat % GRID_W == 0
    tpb, lat_tiles = p_len // ROW_TILE, n_lat // ROW_TILE
    geom = (n_batch, tpb, lat_tiles)

    stream = jnp.concatenate([x, ctx], axis=1).reshape(t, d)
    pad_rows = (-(n_batch + 1)) % 8
    cond = jnp.concatenate([c, c_ctx[None, :], jnp.zeros((pad_rows, d), f32)], axis=0)
    mods = ada_modulation(cond, ada_w, ada_b)[:, :n_batch + 1].reshape(depth, n_batch + 1, N_MOD, d)
    cos_t, sin_t = rope_tables(n_lat, p_len)

    wts = prepare_weights(w_in, mla_wq_up, mla_wkv_up, w_branch, w_out, exp_w1, exp_b1, exp_w2, exp_b2, d)
    bias_tabs = [na_bias_tables(na_rpb[l], n_lat // GRID_W) for l in range(depth)]

    for l in range(depth):
        mod = mods[l]
        (h,) = norm_modulate(stream, norm_mix_g[l], mod, 0, geom, (bf16,))
        o_a, o_b, o_c, o_d, gate_logits = token_mixer_branches(
            h, l, wts, geom, n_lat, cos_t, sin_t, bias_tabs[l], mla_qn_g[l], mla_kvn_g[l], lru_conv_w[l],
            lru_conv_b[l], lru_wa[l], lru_ba[l], lru_wx[l], lru_bx[l], lru_lam[l], dn_conv_w[l], dn_a_log[l],
            dn_dt_bias[l], dn_norm_g[l])

        merged = merge_branches((o_a.reshape(t, -1), o_b.reshape(t, -1), o_c.reshape(t, -1), o_d.reshape(t, -1)),
                                gate_logits, wts["w_branch"], l)
        stream = out_proj_residual(merged, wts["w_out"], l, stream, mod, 2, geom)

        (h2,) = norm_modulate(stream, norm_ffn_g[l], mod, 3, geom, (f32,))
        stream = moe_ffn_residual(h2, stream, mod, 5, geom, router_w[l], router_b[l], l,
                                  wts["exp_w1"], wts["exp_b1"], wts["exp_w2"], wts["exp_b2"])

    out = final_norm(stream, final_g)
    return out.reshape(n_batch, p_len, d)[:, :n_lat]
```

```python
import functools
import math

import jax
import jax.numpy as jnp
from jax import lax
from jax.experimental import pallas as pl
from jax.experimental.pallas import tpu as pltpu

GRID_W = 64
EPS = 1e-6
N_MOD = 6
N_BRANCH = 4

MLA_HEADS = 8
MLA_NOPE = 128
MLA_ROPE = 64
MLA_V = 128
Q_LORA = 512
KV_LORA = 512
ROPE_BASE = 10000.0

NA_HEADS = 8
NA_DIM = 128
WIN_R = 8
WIN_C = 16

LRU_W = 1024
LRU_BLOCKS = 8
LRU_C = 8.0
CONV_W = 4
CONV_PAD_L = 2

DN_HEADS = 8
DN_DK = 128
DN_DV = 128
DN_CHUNK = 64

N_EXPERTS = 32
TOP_K = 4
D_FF = 640
SWIGLU_ALPHA = 1.702
SWIGLU_LIMIT = 7.0

ROW_TILE = 256
LANES = 128
NEG_BIG = -1e30
VMEM_LIMIT = 56 * 1024 * 1024

f32 = jnp.float32
bf16 = jnp.bfloat16


def _params(sem, vmem=VMEM_LIMIT):
    return pltpu.CompilerParams(dimension_semantics=sem, vmem_limit_bytes=vmem)


def _sigmoid(x):
    return 1.0 / (1.0 + jnp.exp(-x))


def _silu(x):
    return x * _sigmoid(x)


def _softplus(x):
    return jnp.maximum(x, 0.0) + jnp.log1p(jnp.exp(-jnp.abs(x)))


def _gelu_tanh(x):
    return 0.5 * x * (1.0 + jnp.tanh(math.sqrt(2.0 / math.pi) * (x + 0.044715 * x * x * x)))


def _mm_kernel(x_ref, w_ref, o_ref):
    o_ref[...] = jnp.dot(x_ref[...], w_ref[...], preferred_element_type=f32).astype(o_ref.dtype)


def matmul(x, w, layer, col0, n, out_dtype, tm, tn, name):
    m, k = x.shape
    while m % tm:
        tm //= 2
    tn = min(tn, n)
    assert tm % 8 == 0 and n % tn == 0 and col0 % tn == 0, (m, n, tm, tn, col0)
    cb = col0 // tn
    return pl.pallas_call(
        _mm_kernel,
        out_shape=jax.ShapeDtypeStruct((m, n), out_dtype),
        grid=(n // tn, m // tm),
        in_specs=[pl.BlockSpec((tm, k), lambda j, i: (i, 0)),
                  pl.BlockSpec((None, k, tn), lambda j, i: (layer, 0, cb + j))],
        out_specs=pl.BlockSpec((tm, tn), lambda j, i: (i, j)),
        compiler_params=_params(("parallel", "parallel")),
        name=name,
    )(x, w)


def _ada_kernel(c_ref, w_ref, b_ref, o_ref):
    a = _silu(c_ref[...])
    o_ref[...] = jnp.dot(a, w_ref[...], preferred_element_type=f32,
                         precision=lax.Precision.HIGHEST) + b_ref[...]


def ada_modulation(cond, ada_w, ada_b):
    depth, d, n = ada_w.shape
    r = cond.shape[0]
    tn = 512
    return pl.pallas_call(
        _ada_kernel,
        out_shape=jax.ShapeDtypeStruct((depth, r, n), f32),
        grid=(depth, n // tn),
        in_specs=[pl.BlockSpec((r, d), lambda l, j: (0, 0)),
                  pl.BlockSpec((None, d, tn), lambda l, j: (l, 0, j)),
                  pl.BlockSpec((None, 1, tn), lambda l, j: (l, 0, j))],
        out_specs=pl.BlockSpec((None, r, tn), lambda l, j: (l, 0, j)),
        compiler_params=_params(("parallel", "parallel")),
        name="ada_modulation",
    )(cond, ada_w, ada_b.reshape(depth, 1, n))


def _mod_row(i, tiles_per_batch, lat_tiles, n_batch):
    return jnp.where(i % tiles_per_batch < lat_tiles, i // tiles_per_batch, n_batch)


def _pack_halves(x):
    m = x.shape[1] // 2
    bits = lambda v: lax.bitcast_convert_type(v.astype(bf16).astype(f32), jnp.uint32)
    return (bits(x[:, :m]) >> 16) | (bits(x[:, m:]) & jnp.uint32(0xFFFF0000))


def _unpack_halves(p):
    return (lax.bitcast_convert_type(p << 16, f32),
            lax.bitcast_convert_type(p & jnp.uint32(0xFFFF0000), f32))


def _norm_mod_kernel(x_ref, g_ref, mod_ref, *o_refs, shift_row):
    x = x_ref[...]
    y = x * lax.rsqrt(jnp.mean(x * x, axis=-1, keepdims=True) + EPS) * g_ref[...]
    h = y * (1.0 + mod_ref[shift_row + 1:shift_row + 2, :]) + mod_ref[shift_row:shift_row + 1, :]
    for o_ref in o_refs:
        o_ref[...] = _pack_halves(h) if o_ref.dtype == jnp.uint32 else h.astype(o_ref.dtype)


def norm_modulate(x, g, mod, shift_row, geom, out_dtypes):
    t, d = x.shape
    n_batch, tpb, lat = geom
    mrow = functools.partial(_mod_row, tiles_per_batch=tpb, lat_tiles=lat, n_batch=n_batch)
    widths = [d // 2 if dt == jnp.uint32 else d for dt in out_dtypes]
    outs = pl.pallas_call(
        functools.partial(_norm_mod_kernel, shift_row=shift_row),
        out_shape=[jax.ShapeDtypeStruct((t, w), dt) for w, dt in zip(widths, out_dtypes)],
        grid=(t // ROW_TILE,),
        in_specs=[pl.BlockSpec((ROW_TILE, d), lambda i: (i, 0)),
                  pl.BlockSpec((1, d), lambda i: (0, 0)),
                  pl.BlockSpec((None, N_MOD, d), lambda i: (mrow(i), 0, 0))],
        out_specs=[pl.BlockSpec((ROW_TILE, w), lambda i: (i, 0)) for w in widths],
        compiler_params=_params(("parallel",)),
        name="norm_modulate",
    )(x, g.reshape(1, d), mod)
    return outs


def _final_norm_kernel(x_ref, g_ref, o_ref):
    x = x_ref[...]
    o_ref[...] = x * lax.rsqrt(jnp.mean(x * x, axis=-1, keepdims=True) + EPS) * g_ref[...]


def final_norm(x, g):
    t, d = x.shape
    return pl.pallas_call(
        _final_norm_kernel,
        out_shape=jax.ShapeDtypeStruct((t, d), f32),
        grid=(t // ROW_TILE,),
        in_specs=[pl.BlockSpec((ROW_TILE, d), lambda i: (i, 0)),
                  pl.BlockSpec((1, d), lambda i: (0, 0))],
        out_specs=pl.BlockSpec((ROW_TILE, d), lambda i: (i, 0)),
        compiler_params=_params(("parallel",)),
        name="final_norm",
    )(x, g.reshape(1, d))


def _mla_prep_kernel(p_ref, qg_ref, kvg_ref, wq_ref, wkv_ref, cos_ref, sin_ref,
                     q_ref, k_ref, v_ref):
    def rms(v, g):
        return v * lax.rsqrt(jnp.mean(v * v, axis=-1, keepdims=True) + EPS) * g

    cos, sin = cos_ref[...], sin_ref[...]

    def rope(t):
        return t * cos + pltpu.roll(t, MLA_ROPE, axis=1) * sin

    qn = rms(p_ref[:, :Q_LORA], qg_ref[...]).astype(bf16)
    kvn = rms(p_ref[:, Q_LORA:Q_LORA + KV_LORA], kvg_ref[...]).astype(bf16)
    scale = (MLA_NOPE + MLA_ROPE) ** -0.5 * math.log2(math.e)
    q = jnp.dot(qn, wq_ref[...], preferred_element_type=f32) * scale
    hw = 2 * LANES
    for h in range(MLA_HEADS):
        q_ref[:, h * hw:h * hw + LANES] = q[:, h * hw:h * hw + LANES].astype(bf16)
        q_ref[:, h * hw + LANES:(h + 1) * hw] = rope(q[:, h * hw + LANES:(h + 1) * hw]).astype(bf16)
    kv = jnp.dot(kvn, wkv_ref[...], preferred_element_type=f32)
    kpe = rope(p_ref[:, Q_LORA + KV_LORA:Q_LORA + KV_LORA + LANES]).astype(bf16)
    nk = MLA_HEADS * MLA_NOPE
    ones_col = jnp.where(lax.broadcasted_iota(jnp.int32, (kpe.shape[0], LANES), 1) == 0, 1.0, 0.0).astype(bf16)
    for h in range(MLA_HEADS):
        k_ref[:, h * hw:h * hw + LANES] = kv[:, h * MLA_NOPE:(h + 1) * MLA_NOPE].astype(bf16)
        k_ref[:, h * hw + LANES:(h + 1) * hw] = kpe
        v_ref[:, h * hw:h * hw + LANES] = kv[:, nk + h * MLA_V:nk + (h + 1) * MLA_V].astype(bf16)
        v_ref[:, h * hw + LANES:(h + 1) * hw] = ones_col


def mla_prep(proj, qn_g, kvn_g, wq, wkv, layer, cos_t, sin_t, tiles_per_batch):
    t = proj.shape[0]
    pw = proj.shape[1]
    qw = MLA_HEADS * 2 * LANES
    return pl.pallas_call(
        _mla_prep_kernel,
        out_shape=[jax.ShapeDtypeStruct((t, qw), bf16),
                   jax.ShapeDtypeStruct((t, qw), bf16),
                   jax.ShapeDtypeStruct((t, qw), bf16)],
        grid=(t // ROW_TILE,),
        in_specs=[pl.BlockSpec((ROW_TILE, pw), lambda i: (i, 0)),
                  pl.BlockSpec((1, Q_LORA), lambda i: (0, 0)),
                  pl.BlockSpec((1, KV_LORA), lambda i: (0, 0)),
                  pl.BlockSpec((None,) + wq.shape[1:], lambda i: (layer, 0, 0)),
                  pl.BlockSpec((None,) + wkv.shape[1:], lambda i: (layer, 0, 0)),
                  pl.BlockSpec((ROW_TILE, LANES), lambda i: (i % tiles_per_batch, 0)),
                  pl.BlockSpec((ROW_TILE, LANES), lambda i: (i % tiles_per_batch, 0))],
        out_specs=[pl.BlockSpec((ROW_TILE, qw), lambda i: (i, 0)),
                   pl.BlockSpec((ROW_TILE, qw), lambda i: (i, 0)),
                   pl.BlockSpec((ROW_TILE, qw), lambda i: (i, 0))],
        compiler_params=_params(("parallel",)),
        name="mla_prep",
    )(proj, qn_g.reshape(1, -1), kvn_g.reshape(1, -1), wq, wkv, cos_t, sin_t)


MLA_Q_TILE = 1024
MLA_SUB_TILE = 256


def _mla_attn_kernel(q_ref, k_ref, v_ref, o_ref):
    for r0 in range(0, q_ref.shape[0], MLA_SUB_TILE):
        rows = slice(r0, r0 + MLA_SUB_TILE)
        s = lax.dot_general(q_ref[rows, :], k_ref[...], (((1,), (1,)), ((), ())), preferred_element_type=f32)
        p = jnp.exp2((s - jnp.max(s, axis=-1, keepdims=True)).astype(bf16))
        ov = jnp.dot(p, v_ref[...], preferred_element_type=f32)
        o_ref[rows, :] = (ov[:, :MLA_V] / ov[:, MLA_V:MLA_V + 1]).astype(o_ref.dtype)


def mla_attention(q, k, v, n_batch, n_lat):
    p_len = q.shape[1]
    n_ctx = p_len - n_lat
    tq = MLA_Q_TILE if n_lat % MLA_Q_TILE == 0 else ROW_TILE
    assert n_lat % n_ctx == 0
    cb = n_lat // n_ctx
    o_lat = pl.pallas_call(
        _mla_attn_kernel,
        out_shape=jax.ShapeDtypeStruct((n_batch, n_lat, MLA_HEADS * MLA_V), bf16),
        grid=(n_batch, MLA_HEADS, n_lat // tq),
        in_specs=[pl.BlockSpec((None, tq, 2 * LANES), lambda b, h, i: (b, i, h)),
                  pl.BlockSpec((None, p_len, 2 * LANES), lambda b, h, i: (b, 0, h)),
                  pl.BlockSpec((None, p_len, 2 * LANES), lambda b, h, i: (b, 0, h))],
        out_specs=pl.BlockSpec((None, tq, MLA_V), lambda b, h, i: (b, i, h)),
        compiler_params=_params(("parallel", "parallel", "arbitrary")),
        name="mla_attention",
    )(q, k, v)
    o_ctx = pl.pallas_call(
        _mla_attn_kernel,
        out_shape=jax.ShapeDtypeStruct((n_batch, n_ctx, MLA_HEADS * MLA_V), bf16),
        grid=(n_batch, MLA_HEADS),
        in_specs=[pl.BlockSpec((None, n_ctx, 2 * LANES), lambda b, h: (b, cb, h)),
                  pl.BlockSpec((None, n_ctx, 2 * LANES), lambda b, h: (b, cb, h)),
                  pl.BlockSpec((None, n_ctx, 2 * LANES), lambda b, h: (b, cb, h))],
        out_specs=pl.BlockSpec((None, n_ctx, MLA_V), lambda b, h: (b, 0, h)),
        compiler_params=_params(("parallel", "parallel")),
        name="mla_attention_ctx",
    )(q, k, v)
    return jnp.concatenate([o_lat, o_ctx], axis=1)


NA_QROWS = 8
NA_BAND = 16


def _na_geometry(rows):
    assert rows % NA_QROWS == 0 and rows >= NA_BAND, rows
    nblk = rows // NA_QROWS
    starts = [min(max(NA_QROWS * i - WIN_R // 2, 0), rows - NA_BAND) for i in range(nblk)]
    if nblk <= 3:
        reps, pat = list(range(nblk)), list(range(nblk))
    else:
        reps = [0, 1, nblk - 1]
        pat = [0] + [1] * (nblk - 2) + [2]
        for i in range(1, nblk - 1):
            assert starts[i] == NA_QROWS * i - WIN_R // 2
    return nblk, starts, reps, pat


def na_bias_tables(rpb, rows):
    _, starts, reps, _ = _na_geometry(rows)
    nh = rpb.shape[0]
    wr = min(WIN_R, rows)
    n_off = 2 * WIN_R - 1
    qc = jnp.arange(GRID_W)
    cs = jnp.clip(qc - WIN_C // 2, 0, GRID_W - WIN_C)
    kc = jnp.arange(GRID_W)
    col_ok = (kc[None, :] >= cs[:, None]) & (kc[None, :] < cs[:, None] + WIN_C)
    col_off = jnp.clip(kc[None, :] - qc[:, None] + (WIN_C - 1), 0, 2 * WIN_C - 2)
    toe = jnp.where(col_ok[None, None], rpb.astype(f32)[:, :, col_off], NEG_BIG)
    toe = jnp.concatenate([toe, jnp.full((nh, 1, GRID_W, GRID_W), NEG_BIG, f32)], axis=1)
    sel = []
    for i in reps:
        qr = NA_QROWS * i + jnp.arange(NA_QROWS)
        kr = starts[i] + jnp.arange(NA_BAND)
        rs = jnp.clip(qr - wr // 2, 0, rows - wr)
        row_ok = (kr[None, :] >= rs[:, None]) & (kr[None, :] < rs[:, None] + wr)
        row_off = jnp.where(row_ok, kr[None, :] - qr[:, None] + (WIN_R - 1), n_off)
        sel.append(jax.nn.one_hot(row_off, n_off + 1, dtype=f32))
    sel = jnp.stack(sel, axis=0)
    tab = jnp.einsum("pakd,hdqc->phaqkc", sel, toe, precision=lax.Precision.HIGHEST)
    return tab.reshape(len(reps), nh, NA_QROWS * GRID_W, NA_BAND * GRID_W)


def _na_kernel(q_ref, k_ref, v_ref, bias_ref, o_ref, *, n_lat, starts, pat):
    scale = NA_DIM ** -0.5
    k_ctx = k_ref[n_lat:, :]
    v_ctx = v_ref[n_lat:, :]
    nt = (((1,), (1,)), ((), ()))
    qt = NA_QROWS * GRID_W
    bt = NA_BAND * GRID_W
    for i, (st, pt) in enumerate(zip(starts, pat)):
        q = q_ref[i * qt:(i + 1) * qt, :]
        kb = k_ref[st * GRID_W:st * GRID_W + bt, :]
        vb = v_ref[st * GRID_W:st * GRID_W + bt, :]
        s_loc = lax.dot_general(q, kb, nt, preferred_element_type=f32) * scale + bias_ref[pt]
        s_ctx = lax.dot_general(q, k_ctx, nt, preferred_element_type=f32) * scale
        m = jnp.maximum(jnp.max(s_loc, axis=-1, keepdims=True), jnp.max(s_ctx, axis=-1, keepdims=True))
        p_loc = jnp.exp(s_loc - m)
        p_ctx = jnp.exp(s_ctx - m)
        l = jnp.sum(p_loc, axis=-1, keepdims=True) + jnp.sum(p_ctx, axis=-1, keepdims=True)
        o = (jnp.dot(p_loc.astype(bf16), vb, preferred_element_type=f32)
             + jnp.dot(p_ctx.astype(bf16), v_ctx, preferred_element_type=f32))
        o_ref[i * qt:(i + 1) * qt, :] = (o / l).astype(o_ref.dtype)
    qz = q_ref[n_lat:, :]
    s = lax.dot_general(qz, k_ctx, nt, preferred_element_type=f32) * scale
    p = jnp.exp(s - jnp.max(s, axis=-1, keepdims=True))
    o = jnp.dot(p.astype(bf16), v_ctx, preferred_element_type=f32) / jnp.sum(p, axis=-1, keepdims=True)
    o_ref[n_lat:, :] = o.astype(o_ref.dtype)


def neighbourhood_attention(qkv, bias, n_batch, n_lat):
    p_len = qkv.shape[1]
    rows = n_lat // GRID_W
    _, starts, _, pat = _na_geometry(rows)
    npat = bias.shape[0]
    kern = functools.partial(_na_kernel, n_lat=n_lat, starts=tuple(starts), pat=tuple(pat))
    hspec = lambda off: pl.BlockSpec((None, p_len, NA_DIM), lambda h, b: (b, 0, off + h))
    return pl.pallas_call(
        kern,
        out_shape=jax.ShapeDtypeStruct((n_batch, p_len, NA_HEADS * NA_DIM), bf16),
        grid=(NA_HEADS, n_batch),
        in_specs=[hspec(0), hspec(NA_HEADS), hspec(2 * NA_HEADS),
                  pl.BlockSpec((npat, None) + bias.shape[2:], lambda h, b: (0, h, 0, 0))],
        out_specs=pl.BlockSpec((None, p_len, NA_DIM), lambda h, b: (b, 0, h)),
        compiler_params=_params(("parallel", "parallel")),
        name="neighbourhood_attention",
    )(qkv, qkv, qkv, bias)


def _segment_conv(x, w_ref, n_lat):
    p_len = x.shape[0]
    row = lax.broadcasted_iota(jnp.int32, (p_len, 1), 0)
    local = jnp.where(row < n_lat, row, row - n_lat)
    seg_len = jnp.where(row < n_lat, n_lat, p_len - n_lat)
    y = x * w_ref[CONV_PAD_L:CONV_PAD_L + 1, :]
    for j in range(CONV_W):
        off = j - CONV_PAD_L
        if off == 0:
            continue
        shifted = pltpu.roll(x, (-off) % p_len, axis=0)
        ok = (local + off >= 0) & (local + off < seg_len)
        y = y + jnp.where(ok, shifted, 0.0) * w_ref[j:j + 1, :]
    return y


SCAN_CHUNK = 256


def _chunk_scan(a, b, reverse):
    n = a.shape[0]
    row = lax.broadcasted_iota(jnp.int32, (n, 1), 0)
    k = 1
    while k < n:
        if reverse:
            a_s = pltpu.roll(a, n - k, axis=0)
            b_s = pltpu.roll(b, n - k, axis=0)
            ok = row < n - k
        else:
            a_s = pltpu.roll(a, k, axis=0)
            b_s = pltpu.roll(b, k, axis=0)
            ok = row >= k
        b = b + a * jnp.where(ok, b_s, 0.0)
        a = a * jnp.where(ok, a_s, 1.0)
        k *= 2
    return a, b


def _lru_kernel(lu_ref, ly_ref, cw_ref, cb_ref, wa_ref, ba_ref, wx_ref, bx_ref, lam_ref, o_ref,
                u_scr, a_scr, b_scr, h_scr, *, n_lat):
    p_len = lu_ref.shape[0]
    u_scr[...] = _segment_conv(lu_ref[...], cw_ref, n_lat) + cb_ref[...]
    segments = ((n_lat, p_len - n_lat), (0, n_lat))

    for d, reverse in enumerate((False, True)):
        ub = u_scr[...].astype(bf16)
        r = _sigmoid(jnp.dot(ub, wa_ref[d].astype(bf16), preferred_element_type=f32) + ba_ref[d:d + 1, :])
        g = _sigmoid(jnp.dot(ub, wx_ref[d].astype(bf16), preferred_element_type=f32) + bx_ref[d:d + 1, :])
        log_a = (-LRU_C * r) * _softplus(-lam_ref[d:d + 1, :])
        a_scr[...] = jnp.exp(log_a)
        th = jnp.tanh(log_a)
        b_scr[...] = jnp.sqrt(-2.0 * th / (1.0 - th)) * (g * u_scr[...])

        state = jnp.zeros((1, LANES), f32)
        for start, length in segments:
            n_chunks = length // SCAN_CHUNK

            def body(c, h_prev, start=start, n_chunks=n_chunks, reverse=reverse):
                ci = (n_chunks - 1 - c) if reverse else c
                rows = pl.ds(pl.multiple_of(start + ci * SCAN_CHUNK, SCAN_CHUNK), SCAN_CHUNK)
                a_cum, h = _chunk_scan(a_scr[rows, :], b_scr[rows, :], reverse)
                h = h + a_cum * h_prev
                if d == 0:
                    h_scr[rows, :] = h
                else:
                    h_scr[rows, :] = h_scr[rows, :] + h
                return h[0:1, :] if reverse else h[SCAN_CHUNK - 1:SCAN_CHUNK, :]

            state = lax.fori_loop(0, n_chunks, body, state)

    o_ref[...] = (h_scr[...] * _gelu_tanh(ly_ref[...])).astype(o_ref.dtype)


def rglru(luy, conv_w, conv_b, wa, ba, wx, bx, lam, n_batch, n_lat):
    p_len = luy.shape[1]
    nblk = LRU_W // LANES
    assert LRU_W // LRU_BLOCKS == LANES and n_lat % SCAN_CHUNK == 0 and (p_len - n_lat) % SCAN_CHUNK == 0
    vec = lambda rows: pl.BlockSpec((rows, LANES), lambda b, j: (0, j))
    return pl.pallas_call(
        functools.partial(_lru_kernel, n_lat=n_lat),
        out_shape=jax.ShapeDtypeStruct((n_batch, p_len, LRU_W), bf16),
        grid=(n_batch, nblk),
        in_specs=[pl.BlockSpec((None, p_len, LANES), lambda b, j: (b, 0, j)),
                  pl.BlockSpec((None, p_len, LANES), lambda b, j: (b, 0, nblk + j)),
                  vec(CONV_W), vec(1),
                  pl.BlockSpec((2, None, LANES, LANES), lambda b, j: (0, j, 0, 0)), vec(2),
                  pl.BlockSpec((2, None, LANES, LANES), lambda b, j: (0, j, 0, 0)), vec(2),
                  vec(2)],
        out_specs=pl.BlockSpec((None, p_len, LANES), lambda b, j: (b, 0, j)),
        scratch_shapes=[pltpu.VMEM((p_len, LANES), f32) for _ in range(4)],
        compiler_params=_params(("parallel", "parallel")),
        name="rglru",
    )(luy, luy, conv_w, conv_b.reshape(1, LRU_W), wa, ba, wx, bx, lam)


def _dn_pre_kernel(x_ref, w_ref, o_ref, *, n_lat):
    j = pl.program_id(1)
    u = _silu(_segment_conv(x_ref[...], w_ref, n_lat))
    nrm = u * lax.rsqrt(jnp.sum(u * u, axis=-1, keepdims=True) + EPS)
    qk_scale = jnp.where(j < DN_HEADS, DN_DK ** -0.5, 1.0)
    o_ref[...] = jnp.where(j < 2 * DN_HEADS, nrm * qk_scale, u).astype(o_ref.dtype)


def deltanet_pre(dqkv, conv_w, n_batch, n_lat, col_blocks):
    p_len = dqkv.shape[1]
    return pl.pallas_call(
        functools.partial(_dn_pre_kernel, n_lat=n_lat),
        out_shape=jax.ShapeDtypeStruct((n_batch, p_len, col_blocks * LANES), bf16),
        grid=(n_batch, col_blocks),
        in_specs=[pl.BlockSpec((None, p_len, LANES), lambda b, j: (b, 0, j)),
                  pl.BlockSpec((CONV_W, LANES), lambda b, j: (0, j))],
        out_specs=pl.BlockSpec((None, p_len, LANES), lambda b, j: (b, 0, j)),
        compiler_params=_params(("parallel", "parallel")),
        name="deltanet_pre",
    )(dqkv, conv_w)


def _dn_gate_kernel(ab_ref, alog_ref, dt_ref, o_ref):
    nh2 = 2 * DN_HEADS
    a_raw = ab_ref[:, :nh2]
    b_raw = ab_ref[:, nh2:2 * nh2]
    g = -jnp.exp(alog_ref[...]) * _softplus(a_raw + dt_ref[...])
    beta = _sigmoid(b_raw)
    n = g.shape[0]
    r = lax.broadcasted_iota(jnp.int32, (n, n), 0)
    c = lax.broadcasted_iota(jnp.int32, (n, n), 1)
    same = (r // DN_CHUNK) == (c // DN_CHUNK)
    hi = lax.Precision.HIGHEST
    pre = jnp.dot(jnp.where(same & (c <= r), 1.0, 0.0), g, preferred_element_type=f32, precision=hi)
    suf = jnp.dot(jnp.where(same & (c >= r), 1.0, 0.0), g, preferred_element_type=f32, precision=hi)
    tot = jnp.dot(jnp.where(same, 1.0, 0.0), g, preferred_element_type=f32, precision=hi)
    pad = jnp.zeros((n, LANES - 3 * nh2), f32)
    o_ref[...] = jnp.concatenate([pre[:, :DN_HEADS], suf[:, DN_HEADS:], beta, tot, pad], axis=-1)


def deltanet_gates(ab, col_block, a_log, dt_bias):
    t = ab.shape[0]
    nh2 = 2 * DN_HEADS
    return pl.pallas_call(
        _dn_gate_kernel,
        out_shape=jax.ShapeDtypeStruct((t, LANES), f32),
        grid=(t // ROW_TILE,),
        in_specs=[pl.BlockSpec((ROW_TILE, LANES), lambda i: (i, col_block)),
                  pl.BlockSpec((1, nh2), lambda i: (0, 0)),
                  pl.BlockSpec((1, nh2), lambda i: (0, 0))],
        out_specs=pl.BlockSpec((ROW_TILE, LANES), lambda i: (i, 0)),
        compiler_params=_params(("parallel",)),
        name="deltanet_gates",
    )(ab, a_log.reshape(1, nh2), dt_bias.reshape(1, nh2))


def _tri_inverse_minus_eye(low):
    c = low.shape[-1]
    bmm = functools.partial(jnp.einsum, "gij,gjk->gik", preferred_element_type=f32)
    x = -low
    acc = x
    k = 2
    while k < c:
        xb = x.astype(bf16)
        x = bmm(xb, xb)
        acc = acc + x + bmm(x.astype(bf16), acc.astype(bf16))
        k *= 2
    return acc


DN_GROUP = 4


def _dn_kernel(q_ref, k_ref, v_ref, z_ref, gcol_ref, grow_ref, ng_ref, o_ref,
               wkq_scr, wv_scr, kd_scr, ain_scr, gend_scr, acc_scr, *, n_lat):
    p_len = q_ref.shape[0]
    cs = DN_CHUNK
    n_chunks = p_len // cs
    nc_lat = n_lat // cs
    nc_ctx = n_chunks - nc_lat
    tn = (((0,), (0,)), ((), ()))
    grp = DN_GROUP
    nb = 2 * grp
    ri = lax.broadcasted_iota(jnp.int32, (nb, cs, cs), 1)
    ci = lax.broadcasted_iota(jnp.int32, (nb, cs, cs), 2)
    rev = lax.broadcasted_iota(jnp.int32, (nb, cs, cs), 0) >= grp
    ahead = jnp.where(rev, ri - ci, ci - ri)
    incl = ahead <= 0
    strict = ahead < 0

    def phase1(g, carry):
        c0 = g * grp
        rows = pl.ds(pl.multiple_of(c0 * cs, cs * grp), cs * grp)

        def both(x):
            x = x.astype(f32).reshape(grp, cs, x.shape[-1])
            return jnp.concatenate([x, x], axis=0)

        def per_dir(col):
            return jnp.concatenate([gcol_ref[rows, col + d:col + d + 1].reshape(grp, cs, 1) for d in range(2)], axis=0)

        q, k, v = both(q_ref[rows, :]), both(k_ref[rows, :]), both(v_ref[rows, :])
        gc, beta, gtot = per_dir(0), per_dir(2), per_dir(4)
        gr = jnp.concatenate([grow_ref[d, pl.ds(c0, grp), :] for d in range(2)], axis=0)[:, None, :]
        decay = jnp.where(incl, jnp.exp(jnp.where(incl, gc - gr, 0.0)), 0.0)
        e_gc = jnp.exp(gc)
        kb = k * beta
        gram = jnp.einsum("gik,gjk->gij", jnp.concatenate([kb, q], axis=1).astype(bf16), k.astype(bf16),
                          preferred_element_type=f32)
        t_m1 = _tri_inverse_minus_eye(jnp.where(strict, gram[:, :cs] * decay, 0.0))
        rhs = jnp.concatenate([v * beta, kb * e_gc], axis=2)
        w = rhs + jnp.einsum("gij,gjk->gik", t_m1.astype(bf16), rhs.astype(bf16), preferred_element_type=f32)
        q_dec = (q * e_gc).astype(bf16)
        k_dec = (k * jnp.exp(gtot - gc)).astype(bf16)
        a_in = (gram[:, cs:] * decay).astype(bf16)
        g_end = jnp.broadcast_to(jnp.exp(gtot[:, 0:1, :]), (nb, 1, DN_DV))
        for d in range(2):
            sl = slice(d * grp, (d + 1) * grp)
            dst = pl.ds(c0, grp)
            wv_scr[d, dst] = w[sl, :, :DN_DV]
            wkq_scr[d, dst, :cs, :] = w[sl, :, DN_DV:].astype(bf16)
            wkq_scr[d, dst, cs:, :] = q_dec[sl]
            kd_scr[d, dst] = k_dec[sl]
            ain_scr[d, dst] = a_in[sl]
            gend_scr[d, dst] = g_end[sl]
        return carry

    lax.fori_loop(0, n_chunks // grp, phase1, 0)
    acc_scr[...] = jnp.zeros_like(acc_scr)

    def phase2(t, states):
        in_ctx = t < nc_ctx
        chunk = (jnp.where(in_ctx, nc_lat + t, t - nc_ctx),
                 jnp.where(in_ctx, n_chunks - 1 - t, nc_lat - 1 - (t - nc_ctx)))
        new_states = []
        for d in range(2):
            c = chunk[d]
            s = states[d]
            a = jnp.dot(wkq_scr[d, c], s.astype(bf16), preferred_element_type=f32)
            v_new = (wv_scr[d, c] - a[:cs]).astype(bf16)
            o = a[cs:] + jnp.dot(ain_scr[d, c], v_new, preferred_element_type=f32)
            rows = pl.ds(pl.multiple_of(c * cs, cs), cs)
            acc_scr[rows, :] = acc_scr[rows, :] + o
            new_states.append(s * gend_scr[d, c] + lax.dot_general(kd_scr[d, c], v_new, tn,
                                                                    preferred_element_type=f32))
        return tuple(new_states)

    zero = jnp.zeros((DN_DK, DN_DV), f32)
    lax.fori_loop(0, n_chunks, phase2, (zero, zero))

    o = acc_scr[...]
    y = o * lax.rsqrt(jnp.mean(o * o, axis=-1, keepdims=True) + EPS) * ng_ref[...]
    o_ref[...] = (y * _silu(z_ref[...])).astype(o_ref.dtype)


def deltanet(qkv, dz_src, z_col0, gcol, grow, norm_g, n_batch, n_lat):
    p_len = qkv.shape[1]
    n_chunks = p_len // DN_CHUNK
    assert n_chunks % DN_GROUP == 0 and n_lat % DN_CHUNK == 0
    hspec = lambda off: pl.BlockSpec((None, p_len, LANES), lambda b, h: (b, 0, off + h))
    return pl.pallas_call(
        functools.partial(_dn_kernel, n_lat=n_lat),
        out_shape=jax.ShapeDtypeStruct((n_batch, p_len, DN_HEADS * DN_DV), bf16),
        grid=(n_batch, DN_HEADS),
        in_specs=[hspec(0), hspec(DN_HEADS), hspec(2 * DN_HEADS),
                  pl.BlockSpec((None, p_len, LANES), lambda b, h: (b, 0, z_col0 + h)),
                  pl.BlockSpec((None, None, p_len, 8), lambda b, h: (b, h, 0, 0)),
                  pl.BlockSpec((None, None, 2, p_len // DN_CHUNK, DN_CHUNK), lambda b, h: (b, h, 0, 0, 0)),
                  pl.BlockSpec((1, DN_DV), lambda b, h: (0, 0))],
        out_specs=pl.BlockSpec((None, p_len, LANES), lambda b, h: (b, 0, h)),
        scratch_shapes=[pltpu.VMEM((2, n_chunks, 2 * DN_CHUNK, DN_DK), bf16),
                        pltpu.VMEM((2, n_chunks, DN_CHUNK, DN_DV), f32),
                        pltpu.VMEM((2, n_chunks, DN_CHUNK, DN_DK), bf16),
                        pltpu.VMEM((2, n_chunks, DN_CHUNK, DN_CHUNK), bf16),
                        pltpu.VMEM((2, n_chunks, 1, DN_DV), f32),
                        pltpu.VMEM((p_len, DN_DV), f32)],
        compiler_params=_params(("parallel", "parallel")),
        name="deltanet",
    )(qkv, qkv, qkv, dz_src, gcol, grow, norm_g.reshape(1, DN_DV))


def _merge_kernel(*refs):
    o_refs, h_ref, wg_refs, wb_ref, out_ref = (refs[:N_BRANCH], refs[N_BRANCH], refs[N_BRANCH + 1:2 * N_BRANCH + 1],
                                               refs[-2], refs[-1])
    h = h_ref[...]
    acc = None
    for i in range(N_BRANCH):
        y = jnp.dot(o_refs[i][...], wb_ref[i], preferred_element_type=f32)
        y = _sigmoid(jnp.dot(h, wg_refs[i][...], preferred_element_type=f32)) * y
        acc = y if acc is None else acc + y
    out_ref[...] = acc.astype(out_ref.dtype)


def merge_branches(branches, h, w_all, off_gate, w_branch, layer):
    t, bw = branches[0].shape
    d = w_branch.shape[3]
    tm, tn = ROW_TILE * 2, min(512, d)
    while t % tm:
        tm //= 2
    nj = d // tn
    assert off_gate % tn == 0 and d % tn == 0
    bspec = pl.BlockSpec((tm, bw), lambda j, i: (i, 0))
    gspecs = [pl.BlockSpec((None, d, tn),
                           functools.partial(lambda j, i, cb: (layer, 0, cb + j), cb=(off_gate + br * d) // tn))
              for br in range(N_BRANCH)]
    return pl.pallas_call(
        _merge_kernel,
        out_shape=jax.ShapeDtypeStruct((t, d), bf16),
        grid=(nj, t // tm),
        in_specs=([bspec] * N_BRANCH + [pl.BlockSpec((tm, d), lambda j, i: (i, 0))] + gspecs
                  + [pl.BlockSpec((None, N_BRANCH, bw, tn), lambda j, i: (layer, 0, 0, j))]),
        out_specs=pl.BlockSpec((tm, tn), lambda j, i: (i, j)),
        compiler_params=_params(("parallel", "parallel")),
        name="merge_branches",
    )(*branches, h, *([w_all] * N_BRANCH), w_branch)


def _out_proj_kernel(m_ref, w_ref, x_ref, mod_ref, o_ref, *, gate_row):
    y = jnp.dot(m_ref[...], w_ref[...], preferred_element_type=f32)
    o_ref[...] = x_ref[...] + mod_ref[gate_row:gate_row + 1, :] * y


def out_proj_residual(merged, w_out, layer, x, mod, gate_row, geom):
    t, d = x.shape
    n_batch, tpb, lat = geom
    mrow = functools.partial(_mod_row, tiles_per_batch=tpb, lat_tiles=lat, n_batch=n_batch)
    return pl.pallas_call(
        functools.partial(_out_proj_kernel, gate_row=gate_row),
        out_shape=jax.ShapeDtypeStruct((t, d), f32),
        grid=(t // ROW_TILE,),
        in_specs=[pl.BlockSpec((ROW_TILE, d), lambda i: (i, 0)),
                  pl.BlockSpec((None, d, d), lambda i: (layer, 0, 0)),
                  pl.BlockSpec((ROW_TILE, d), lambda i: (i, 0)),
                  pl.BlockSpec((None, N_MOD, d), lambda i: (mrow(i), 0, 0))],
        out_specs=pl.BlockSpec((ROW_TILE, d), lambda i: (i, 0)),
        compiler_params=_params(("parallel",)),
        name="out_proj_residual",
    )(merged, w_out, x, mod)


def _route_kernel(h_ref, rw_ref, rb_ref, sel_ref, cnt_ref, carry):
    i = pl.program_id(0)

    @pl.when(i == 0)
    def _():
        carry[...] = jnp.zeros_like(carry)

    logits = jnp.dot(h_ref[...], rw_ref[...], preferred_element_type=f32,
                     precision=lax.Precision.HIGHEST) + rb_ref[...]
    n = logits.shape[0]
    lane = lax.broadcasted_iota(jnp.int32, (n, LANES), 1).astype(f32)
    work = logits
    vals, idxs, hots = [], [], []
    for _ in range(TOP_K):
        m = jnp.max(work, axis=-1, keepdims=True)
        idx = jnp.min(jnp.where(work == m, lane, float(LANES)), axis=-1, keepdims=True)
        hot = lane == idx
        vals.append(m)
        idxs.append(idx)
        hots.append(hot)
        work = jnp.where(hot, NEG_BIG * 2.0, work)
    exps = [jnp.exp(v - vals[0]) for v in vals]
    den = exps[0]
    for e in exps[1:]:
        den = den + e
    onehot = jnp.zeros((n, LANES), f32)
    for hot in hots:
        onehot = onehot + jnp.where(hot, 1.0, 0.0)
    r = lax.broadcasted_iota(jnp.int32, (n, n), 0)
    c = lax.broadcasted_iota(jnp.int32, (n, n), 1)
    before = jnp.dot(jnp.where(c < r, 1.0, 0.0).astype(bf16), onehot.astype(bf16),
                     preferred_element_type=f32) + carry[...]
    out = jnp.zeros((n, LANES), f32)
    for kk in range(TOP_K):
        rank = jnp.sum(jnp.where(hots[kk], before, 0.0), axis=-1, keepdims=True)
        out = jnp.where(lane == float(kk), idxs[kk], out)
        out = jnp.where(lane == float(TOP_K + kk), rank, out)
        out = jnp.where(lane == float(2 * TOP_K + kk), exps[kk] / den, out)
    sel_ref[...] = out
    carry[...] = carry[...] + jnp.sum(onehot, axis=0, keepdims=True)
    cnt_ref[...] = carry[...]


def moe_route(h, router_w, router_b):
    t, d = h.shape
    e = router_w.shape[1]
    rw = jnp.zeros((d, LANES), f32).at[:, :e].set(router_w)
    rb = jnp.full((1, LANES), NEG_BIG, f32).at[0, :e].set(router_b)
    return pl.pallas_call(
        _route_kernel,
        out_shape=[jax.ShapeDtypeStruct((t, LANES), f32), jax.ShapeDtypeStruct((1, LANES), f32)],
        grid=(t // ROW_TILE,),
        in_specs=[pl.BlockSpec((ROW_TILE, d), lambda i: (i, 0)),
                  pl.BlockSpec((d, LANES), lambda i: (0, 0)),
                  pl.BlockSpec((1, LANES), lambda i: (0, 0))],
        out_specs=[pl.BlockSpec((ROW_TILE, LANES), lambda i: (i, 0)),
                   pl.BlockSpec((1, LANES), lambda i: (0, 0))],
        scratch_shapes=[pltpu.VMEM((1, LANES), f32)],
        compiler_params=_params(("arbitrary",)),
        name="moe_route",
    )(h, rw, rb)


def _dispatch_kernel(pos_hbm, x_ref, init_ref, xs_hbm, pos_smem, sem_pos, sem_rows):
    del init_ref
    i = pl.program_id(0)
    n_sel = ROW_TILE * TOP_K
    cp = pltpu.make_async_copy(pos_hbm.at[i], pos_smem, sem_pos)
    cp.start()
    cp.wait()

    def issue(r, c):
        src = x_ref.at[pl.ds(r, 1)]
        for kk in range(TOP_K):
            pltpu.make_async_copy(src, xs_hbm.at[pl.ds(pos_smem[r * TOP_K + kk], 1)], sem_rows).start()
        return c

    lax.fori_loop(0, ROW_TILE, issue, 0, unroll=4)
    pltpu.make_async_copy(xs_hbm.at[pl.ds(0, n_sel)], xs_hbm.at[pl.ds(0, n_sel)], sem_rows).wait()


def moe_dispatch(h, pos, n_slots):
    t, d = h.shape
    tiles = t // ROW_TILE
    init = jnp.zeros((n_slots, d), h.dtype)
    return pl.pallas_call(
        _dispatch_kernel,
        out_shape=jax.ShapeDtypeStruct((n_slots, d), h.dtype),
        grid=(tiles,),
        in_specs=[pl.BlockSpec(memory_space=pl.ANY),
                  pl.BlockSpec((ROW_TILE, d), lambda i: (i, 0)),
                  pl.BlockSpec(memory_space=pl.ANY)],
        out_specs=pl.BlockSpec(memory_space=pl.ANY),
        scratch_shapes=[pltpu.SMEM((ROW_TILE * TOP_K,), jnp.int32),
                        pltpu.SemaphoreType.DMA, pltpu.SemaphoreType.DMA],
        input_output_aliases={2: 0},
        compiler_params=_params(("arbitrary",)),
        name="moe_dispatch",
    )(pos.reshape(tiles, ROW_TILE * TOP_K), h, init)


def _expert_kernel(te_ref, xs_ref, w1_ref, b1_ref, sel_ref, w2_ref, b2_ref, ys_ref):
    del te_ref
    x = jnp.concatenate(_unpack_halves(xs_ref[...]), axis=1).astype(bf16)
    u = jnp.dot(x, w1_ref[...], preferred_element_type=f32) + b1_ref[...]
    n = u.shape[1]
    lin = pltpu.roll(u, n - 1, axis=1)
    glu = jnp.minimum(u, SWIGLU_LIMIT)
    lin = jnp.clip(lin, -SWIGLU_LIMIT, SWIGLU_LIMIT)
    act = glu * _sigmoid(SWIGLU_ALPHA * glu) * (lin + 1.0)
    even = lax.broadcasted_iota(jnp.int32, u.shape, 1) % 2 == 0
    act = jnp.where(even, act, 0.0).astype(bf16)
    act = jnp.dot(act, sel_ref[...], preferred_element_type=f32).astype(bf16)
    ys_ref[...] = _pack_halves(jnp.dot(act, w2_ref[...], preferred_element_type=f32) + b2_ref[...])


EXPERT_TILE = 512


def moe_experts(xs, tile_expert, layer, w1, b1, w2, b2):
    s, dh = xs.shape
    d = 2 * dh
    ff2 = w1.shape[3]
    ff = ff2 // 2
    sel = (jnp.arange(ff2)[:, None] == 2 * jnp.arange(ff)[None, :]).astype(bf16)
    grid_spec = pltpu.PrefetchScalarGridSpec(
        num_scalar_prefetch=1,
        grid=(s // EXPERT_TILE,),
        in_specs=[pl.BlockSpec((EXPERT_TILE, dh), lambda i, te: (i, 0)),
                  pl.BlockSpec((None, None, d, ff2), lambda i, te: (layer, te[i], 0, 0)),
                  pl.BlockSpec((None, None, 1, ff2), lambda i, te: (layer, te[i], 0, 0)),
                  pl.BlockSpec((ff2, ff), lambda i, te: (0, 0)),
                  pl.BlockSpec((None, None, ff, d), lambda i, te: (layer, te[i], 0, 0)),
                  pl.BlockSpec((None, None, 1, d), lambda i, te: (layer, te[i], 0, 0))],
        out_specs=pl.BlockSpec((EXPERT_TILE, dh), lambda i, te: (i, 0)),
    )
    return pl.pallas_call(
        _expert_kernel,
        out_shape=jax.ShapeDtypeStruct((s, dh), jnp.uint32),
        grid_spec=grid_spec,
        compiler_params=_params(("arbitrary",)),
        name="moe_experts",
    )(tile_expert, xs, w1, b1, sel, w2, b2)


def _combine_kernel(pos_hbm, ys_hbm, x_ref, wt_ref, mod_ref, o_ref, buf, pos_smem, sem_pos, sem_rows, *, gate_row):
    i = pl.program_id(0)
    n_sel = ROW_TILE * TOP_K
    cp = pltpu.make_async_copy(pos_hbm.at[i], pos_smem, sem_pos)
    cp.start()
    cp.wait()

    def issue(r, c):
        for kk in range(TOP_K):
            pltpu.make_async_copy(ys_hbm.at[pl.ds(pos_smem[r * TOP_K + kk], 1)],
                                  buf.at[kk, pl.ds(r, 1)], sem_rows).start()
        return c

    lax.fori_loop(0, ROW_TILE, issue, 0, unroll=4)
    pltpu.make_async_copy(ys_hbm.at[pl.ds(0, n_sel)], ys_hbm.at[pl.ds(0, n_sel)], sem_rows).wait()
    m = buf.shape[2]
    acc_lo = acc_hi = None
    for kk in range(TOP_K):
        wk = wt_ref[:, 2 * TOP_K + kk:2 * TOP_K + kk + 1]
        lo, hi = _unpack_halves(buf[kk])
        acc_lo = wk * lo if acc_lo is None else acc_lo + wk * lo
        acc_hi = wk * hi if acc_hi is None else acc_hi + wk * hi
    o_ref[:, :m] = x_ref[:, :m] + mod_ref[gate_row:gate_row + 1, :m] * acc_lo
    o_ref[:, m:] = x_ref[:, m:] + mod_ref[gate_row:gate_row + 1, m:] * acc_hi


def moe_combine(ys, pos, sel, x, mod, gate_row, geom):
    t, d = x.shape
    tiles = t // ROW_TILE
    n_batch, tpb, lat = geom
    mrow = functools.partial(_mod_row, tiles_per_batch=tpb, lat_tiles=lat, n_batch=n_batch)
    return pl.pallas_call(
        functools.partial(_combine_kernel, gate_row=gate_row),
        out_shape=jax.ShapeDtypeStruct((t, d), f32),
        grid=(tiles,),
        in_specs=[pl.BlockSpec(memory_space=pl.ANY),
                  pl.BlockSpec(memory_space=pl.ANY),
                  pl.BlockSpec((ROW_TILE, d), lambda i: (i, 0)),
                  pl.BlockSpec((ROW_TILE, LANES), lambda i: (i, 0)),
                  pl.BlockSpec((None, N_MOD, d), lambda i: (mrow(i), 0, 0))],
        out_specs=pl.BlockSpec((ROW_TILE, d), lambda i: (i, 0)),
        scratch_shapes=[pltpu.VMEM((TOP_K, ROW_TILE, d // 2), jnp.uint32),
                        pltpu.SMEM((ROW_TILE * TOP_K,), jnp.int32),
                        pltpu.SemaphoreType.DMA, pltpu.SemaphoreType.DMA],
        compiler_params=_params(("arbitrary",)),
        name="moe_combine",
    )(pos.reshape(tiles, ROW_TILE * TOP_K), ys, x, sel, mod)


def moe_ffn_residual(h, h_packed, x, mod, gate_row, geom, router_w, router_b, layer, w1, b1, w2, b2):
    t, d = h.shape
    n_exp = router_w.shape[1]
    sel, counts = moe_route(h, router_w, router_b)
    eidx = sel[:, :TOP_K].astype(jnp.int32)
    rank = sel[:, TOP_K:2 * TOP_K].astype(jnp.int32)
    counts = counts[0, :n_exp].astype(jnp.int32)
    padded = ((counts + EXPERT_TILE - 1) // EXPERT_TILE) * EXPERT_TILE
    ends = jnp.cumsum(padded)
    starts = ends - padded
    pos = starts[eidx] + rank
    assert (t * TOP_K) % EXPERT_TILE == 0
    n_tiles = (t * TOP_K) // EXPERT_TILE + n_exp
    tile_start = jnp.arange(n_tiles, dtype=jnp.int32) * EXPERT_TILE
    tile_expert = jnp.minimum(jnp.sum((ends[None, :] <= tile_start[:, None]).astype(jnp.int32), axis=1), n_exp - 1)
    xs = moe_dispatch(h_packed, pos, n_tiles * EXPERT_TILE)
    ys = moe_experts(xs, tile_expert, layer, w1, b1, w2, b2)
    return moe_combine(ys, pos, sel, x, mod, gate_row, geom)


def _rope_perm(w):
    lead = w.shape[:-1]
    nf = MLA_ROPE // 4
    wr = w.reshape(*lead, 2, 2, nf)
    return jnp.stack([-wr[..., 1, :], wr[..., 0, :]], axis=-2).reshape(*lead, MLA_ROPE)


NA_COLS = 3 * NA_HEADS * NA_DIM
LRU_COLS = 2 * LRU_W
DN_COLS = DN_HEADS * (2 * DN_DK + DN_DV) + DN_HEADS * DN_DV
MLA_COLS = Q_LORA + KV_LORA + 2 * LANES


def regroup_w_in(w_in, d_model):
    sizes = (Q_LORA, KV_LORA, MLA_ROPE, NA_HEADS * NA_DIM, NA_HEADS * NA_DIM, NA_HEADS * NA_DIM, LRU_W, LRU_W,
             DN_HEADS * (2 * DN_DK + DN_DV), DN_HEADS * DN_DV, 2 * DN_HEADS, 2 * DN_HEADS, N_BRANCH * d_model)
    parts, s = [], 0
    for n in sizes:
        parts.append(w_in[:, :, s:s + n])
        s += n
    assert s == w_in.shape[2]
    (w_qc, w_kvc, w_kpe, w_naq, w_nak, w_nav, w_lu, w_ly, w_dqkv, w_dz, w_da, w_db, w_gt) = parts
    pad = jnp.zeros(w_in.shape[:2] + (LANES - 4 * DN_HEADS,), w_in.dtype)
    w = jnp.concatenate([w_naq, w_nak, w_nav, w_lu, w_ly, w_dqkv, w_dz, w_gt,
                         w_qc, w_kvc, w_kpe, _rope_perm(w_kpe), w_da, w_db, pad], axis=2).astype(bf16)
    off_na = 0
    off_lru = off_na + NA_COLS
    off_dn = off_lru + LRU_COLS
    off_gate = off_dn + DN_COLS
    off_mla = off_gate + N_BRANCH * d_model
    assert w.shape[2] == off_mla + MLA_COLS
    return w, (off_na, off_lru, off_dn, off_gate, off_mla)


def rope_tables(n_lat, p_len):
    t = jnp.arange(n_lat)
    row = (t // GRID_W).astype(f32)
    col = (t % GRID_W).astype(f32)
    nf = MLA_ROPE // 4
    inv_freq = ROPE_BASE ** (-jnp.arange(nf, dtype=f32) / nf)
    ang = jnp.concatenate([row[:, None] * inv_freq] * 2 + [col[:, None] * inv_freq] * 2, axis=1)
    zeros = jnp.zeros((n_lat, LANES - MLA_ROPE), f32)
    cos_t = jnp.concatenate([jnp.cos(ang), zeros], axis=1)
    sin_t = jnp.concatenate([jnp.sin(ang), zeros], axis=1)
    n_ctx = p_len - n_lat
    ctx_cos = jnp.concatenate([jnp.ones((n_ctx, MLA_ROPE), f32), jnp.zeros((n_ctx, LANES - MLA_ROPE), f32)], axis=1)
    return (jnp.concatenate([cos_t, ctx_cos], axis=0),
            jnp.concatenate([sin_t, jnp.zeros((n_ctx, LANES), f32)], axis=0))


def prepare_weights(w_in, mla_wq_up, mla_wkv_up, w_branch, w_out, exp_w1, exp_b1, exp_w2, exp_b2, d_model):
    depth = w_in.shape[0]
    w_all, offs = regroup_w_in(w_in, d_model)
    wq = mla_wq_up.reshape(depth, Q_LORA, MLA_HEADS, MLA_NOPE + MLA_ROPE)
    wq = jnp.concatenate([wq, _rope_perm(wq[..., MLA_NOPE:])], axis=-1).reshape(depth, Q_LORA, -1).astype(bf16)
    wkv = mla_wkv_up.reshape(depth, KV_LORA, MLA_HEADS, MLA_NOPE + MLA_V)
    wkv = jnp.concatenate([wkv[..., :MLA_NOPE].reshape(depth, KV_LORA, -1),
                           wkv[..., MLA_NOPE:].reshape(depth, KV_LORA, -1)], axis=2).astype(bf16)
    return dict(w_all=w_all, offs=offs, wq=wq, wkv=wkv, w_branch=w_branch.astype(bf16), w_out=w_out.astype(bf16),
                exp_w1=exp_w1.astype(bf16), exp_b1=exp_b1[:, :, None, :], exp_w2=exp_w2.astype(bf16),
                exp_b2=exp_b2[:, :, None, :])


def token_mixer_branches(h, l, wts, geom, n_lat, cos_t, sin_t, na_bias, mla_qn_g, mla_kvn_g, lru_conv_w, lru_conv_b,
                         lru_wa, lru_ba, lru_wx, lru_bx, lru_lam, dn_conv_w, dn_a_log, dn_dt_bias, dn_norm_g):
    n_batch, tpb, lat_tiles = geom
    t, d = h.shape
    p_len = t // n_batch
    w_all = wts["w_all"]
    off_na, off_lru, off_dn, off_gate, off_mla = wts["offs"]

    p_mla = matmul(h, w_all, l, off_mla, MLA_COLS, f32, 1024, 256, "proj_mla")
    q, k, v = mla_prep(p_mla, mla_qn_g, mla_kvn_g, wts["wq"], wts["wkv"], l, cos_t, sin_t, tpb)
    o_a = mla_attention(q.reshape(n_batch, p_len, -1), k.reshape(n_batch, p_len, -1),
                        v.reshape(n_batch, p_len, -1), n_batch, n_lat)

    p_na = matmul(h, w_all, l, off_na, NA_COLS, bf16, 1024, 512, "proj_na")
    o_b = neighbourhood_attention(p_na.reshape(n_batch, p_len, -1), na_bias, n_batch, n_lat)

    p_lru = matmul(h, w_all, l, off_lru, LRU_COLS, f32, 1024, 512, "proj_lru")
    o_c = rglru(p_lru.reshape(n_batch, p_len, -1), lru_conv_w, lru_conv_b, lru_wa, lru_ba, lru_wx, lru_bx, lru_lam,
                n_batch, n_lat)

    p_dn = matmul(h, w_all, l, off_dn, DN_COLS, f32, 1024, 512, "proj_dn").reshape(n_batch, p_len, -1)
    qkv_dn = deltanet_pre(p_dn, dn_conv_w, n_batch, n_lat, 3 * DN_HEADS)
    gates = deltanet_gates(p_mla, (MLA_COLS - LANES) // LANES, dn_a_log, dn_dt_bias).reshape(n_batch, p_len, LANES)
    nh = DN_HEADS

    def per_head(cols):
        return jnp.transpose(cols.reshape(n_batch, p_len, 2, nh), (0, 3, 1, 2))

    gc_cols = per_head(gates[..., :2 * nh])
    gcol = jnp.concatenate([gc_cols, per_head(gates[..., 2 * nh:4 * nh]), per_head(gates[..., 4 * nh:6 * nh]),
                            jnp.zeros((n_batch, nh, p_len, 2), f32)], axis=-1)
    grow = jnp.transpose(gc_cols, (0, 1, 3, 2)).reshape(n_batch, nh, 2, p_len // DN_CHUNK, DN_CHUNK)
    o_d = deltanet(qkv_dn, p_dn, 3 * DN_HEADS, gcol, grow, dn_norm_g, n_batch, n_lat)

    return o_a, o_b, o_c, o_d


def kernel(x, c, ctx, c_ctx, ada_w, ada_b, norm_mix_g, norm_ffn_g, w_in, mla_qn_g, mla_wq_up, mla_kvn_g, mla_wkv_up, na_rpb, lru_conv_w, lru_conv_b, lru_wa, lru_ba, lru_wx, lru_bx, lru_lam, dn_conv_w, dn_a_log, dn_dt_bias, dn_norm_g, w_branch, w_out, router_w, router_b, exp_w1, exp_b1, exp_w2, exp_b2, final_g):
    n_batch, n_lat, d = x.shape
    n_ctx = ctx.shape[1]
    depth = ada_w.shape[0]
    p_len = n_lat + n_ctx
    t = n_batch * p_len
    assert n_lat % ROW_TILE == 0 and n_ctx % ROW_TILE == 0 and n_lat % GRID_W == 0
    tpb, lat_tiles = p_len // ROW_TILE, n_lat // ROW_TILE
    geom = (n_batch, tpb, lat_tiles)

    stream = jnp.concatenate([x, ctx], axis=1).reshape(t, d)
    pad_rows = (-(n_batch + 1)) % 8
    cond = jnp.concatenate([c, c_ctx[None, :], jnp.zeros((pad_rows, d), f32)], axis=0)
    mods = ada_modulation(cond, ada_w, ada_b)[:, :n_batch + 1].reshape(depth, n_batch + 1, N_MOD, d)
    cos_t, sin_t = rope_tables(n_lat, p_len)

    wts = prepare_weights(w_in, mla_wq_up, mla_wkv_up, w_branch, w_out, exp_w1, exp_b1, exp_w2, exp_b2, d)
    bias_tabs = [na_bias_tables(na_rpb[l], n_lat // GRID_W) for l in range(depth)]

    for l in range(depth):
        mod = mods[l]
        (h,) = norm_modulate(stream, norm_mix_g[l], mod, 0, geom, (bf16,))
        o_a, o_b, o_c, o_d = token_mixer_branches(
            h, l, wts, geom, n_lat, cos_t, sin_t, bias_tabs[l], mla_qn_g[l], mla_kvn_g[l], lru_conv_w[l],
            lru_conv_b[l], lru_wa[l], lru_ba[l], lru_wx[l], lru_bx[l], lru_lam[l], dn_conv_w[l], dn_a_log[l],
            dn_dt_bias[l], dn_norm_g[l])

        merged = merge_branches((o_a.reshape(t, -1), o_b.reshape(t, -1), o_c.reshape(t, -1), o_d.reshape(t, -1)),
                                h, wts["w_all"], wts["offs"][3], wts["w_branch"], l)
        stream = out_proj_residual(merged, wts["w_out"], l, stream, mod, 2, geom)

        h2, h2_packed = norm_modulate(stream, norm_ffn_g[l], mod, 3, geom, (f32, jnp.uint32))
        stream = moe_ffn_residual(h2, h2_packed, stream, mod, 5, geom, router_w[l], router_b[l], l,
                                  wts["exp_w1"], wts["exp_b1"], wts["exp_w2"], wts["exp_b2"])

    out = final_norm(stream, final_g)
    return out.reshape(n_batch, p_len, d)[:, :n_lat]
```

```python
import functools
import math

import jax
import jax.numpy as jnp
from jax import lax
from jax.experimental import pallas as pl
from jax.experimental.pallas import tpu as pltpu

GRID_W = 64
EPS = 1e-6
N_MOD = 6
N_BRANCH = 4

MLA_HEADS = 8
MLA_NOPE = 128
MLA_ROPE = 64
MLA_V = 128
Q_LORA = 512
KV_LORA = 512
ROPE_BASE = 10000.0

NA_HEADS = 8
NA_DIM = 128
WIN_R = 8
WIN_C = 16

LRU_W = 1024
LRU_BLOCKS = 8
LRU_C = 8.0
CONV_W = 4
CONV_PAD_L = 2

DN_HEADS = 8
DN_DK = 128
DN_DV = 128
DN_CHUNK = 64

N_EXPERTS = 32
TOP_K = 4
D_FF = 640
SWIGLU_ALPHA = 1.702
SWIGLU_LIMIT = 7.0

ROW_TILE = 256
LANES = 128
NEG_BIG = -1e30
VMEM_LIMIT = 56 * 1024 * 1024

f32 = jnp.float32
bf16 = jnp.bfloat16


def _params(sem, vmem=VMEM_LIMIT):
    return pltpu.CompilerParams(dimension_semantics=sem, vmem_limit_bytes=vmem)


def _sigmoid(x):
    return 1.0 / (1.0 + jnp.exp(-x))


def _silu(x):
    return x * _sigmoid(x)


def _softplus(x):
    return jnp.maximum(x, 0.0) + jnp.log1p(jnp.exp(-jnp.abs(x)))


def _gelu_tanh(x):
    return 0.5 * x * (1.0 + jnp.tanh(math.sqrt(2.0 / math.pi) * (x + 0.044715 * x * x * x)))


def _mm_kernel(x_ref, w_ref, o_ref):
    o_ref[...] = jnp.dot(x_ref[...], w_ref[...], preferred_element_type=f32).astype(o_ref.dtype)


def matmul(x, w, layer, col0, n, out_dtype, tm, tn, name):
    m, k = x.shape
    while m % tm:
        tm //= 2
    tn = min(tn, n)
    assert tm % 8 == 0 and n % tn == 0 and col0 % tn == 0, (m, n, tm, tn, col0)
    cb = col0 // tn
    return pl.pallas_call(
        _mm_kernel,
        out_shape=jax.ShapeDtypeStruct((m, n), out_dtype),
        grid=(n // tn, m // tm),
        in_specs=[pl.BlockSpec((tm, k), lambda j, i: (i, 0)),
                  pl.BlockSpec((None, k, tn), lambda j, i: (layer, 0, cb + j))],
        out_specs=pl.BlockSpec((tm, tn), lambda j, i: (i, j)),
        compiler_params=_params(("parallel", "parallel")),
        name=name,
    )(x, w)


def _ada_kernel(c_ref, w_ref, b_ref, o_ref):
    a = _silu(c_ref[...])
    o_ref[...] = jnp.dot(a, w_ref[...], preferred_element_type=f32,
                         precision=lax.Precision.HIGHEST) + b_ref[...]


def ada_modulation(cond, ada_w, ada_b):
    depth, d, n = ada_w.shape
    r = cond.shape[0]
    tn = 512
    return pl.pallas_call(
        _ada_kernel,
        out_shape=jax.ShapeDtypeStruct((depth, r, n), f32),
        grid=(depth, n // tn),
        in_specs=[pl.BlockSpec((r, d), lambda l, j: (0, 0)),
                  pl.BlockSpec((None, d, tn), lambda l, j: (l, 0, j)),
                  pl.BlockSpec((None, 1, tn), lambda l, j: (l, 0, j))],
        out_specs=pl.BlockSpec((None, r, tn), lambda l, j: (l, 0, j)),
        compiler_params=_params(("parallel", "parallel")),
        name="ada_modulation",
    )(cond, ada_w, ada_b.reshape(depth, 1, n))


def _mod_row(i, tiles_per_batch, lat_tiles, n_batch):
    return jnp.where(i % tiles_per_batch < lat_tiles, i // tiles_per_batch, n_batch)


def _pack_halves(x):
    m = x.shape[1] // 2
    bits = lambda v: lax.bitcast_convert_type(v.astype(bf16).astype(f32), jnp.uint32)
    return (bits(x[:, :m]) >> 16) | (bits(x[:, m:]) & jnp.uint32(0xFFFF0000))


def _unpack_halves(p):
    return (lax.bitcast_convert_type(p << 16, f32),
            lax.bitcast_convert_type(p & jnp.uint32(0xFFFF0000), f32))


def _norm_mod_kernel(x_ref, g_ref, mod_ref, *o_refs, shift_row):
    x = x_ref[...]
    y = x * lax.rsqrt(jnp.mean(x * x, axis=-1, keepdims=True) + EPS) * g_ref[...]
    h = y * (1.0 + mod_ref[shift_row + 1:shift_row + 2, :]) + mod_ref[shift_row:shift_row + 1, :]
    for o_ref in o_refs:
        o_ref[...] = _pack_halves(h) if o_ref.dtype == jnp.uint32 else h.astype(o_ref.dtype)


def norm_modulate(x, g, mod, shift_row, geom, out_dtypes):
    t, d = x.shape
    n_batch, tpb, lat = geom
    mrow = functools.partial(_mod_row, tiles_per_batch=tpb, lat_tiles=lat, n_batch=n_batch)
    widths = [d // 2 if dt == jnp.uint32 else d for dt in out_dtypes]
    outs = pl.pallas_call(
        functools.partial(_norm_mod_kernel, shift_row=shift_row),
        out_shape=[jax.ShapeDtypeStruct((t, w), dt) for w, dt in zip(widths, out_dtypes)],
        grid=(t // ROW_TILE,),
        in_specs=[pl.BlockSpec((ROW_TILE, d), lambda i: (i, 0)),
                  pl.BlockSpec((1, d), lambda i: (0, 0)),
                  pl.BlockSpec((None, N_MOD, d), lambda i: (mrow(i), 0, 0))],
        out_specs=[pl.BlockSpec((ROW_TILE, w), lambda i: (i, 0)) for w in widths],
        compiler_params=_params(("parallel",)),
        name="norm_modulate",
    )(x, g.reshape(1, d), mod)
    return outs


def _final_norm_kernel(x_ref, g_ref, o_ref):
    x = x_ref[...]
    o_ref[...] = x * lax.rsqrt(jnp.mean(x * x, axis=-1, keepdims=True) + EPS) * g_ref[...]


def final_norm(x, g):
    t, d = x.shape
    return pl.pallas_call(
        _final_norm_kernel,
        out_shape=jax.ShapeDtypeStruct((t, d), f32),
        grid=(t // ROW_TILE,),
        in_specs=[pl.BlockSpec((ROW_TILE, d), lambda i: (i, 0)),
                  pl.BlockSpec((1, d), lambda i: (0, 0))],
        out_specs=pl.BlockSpec((ROW_TILE, d), lambda i: (i, 0)),
        compiler_params=_params(("parallel",)),
        name="final_norm",
    )(x, g.reshape(1, d))


def _mla_prep_kernel(p_ref, qg_ref, kvg_ref, wq_ref, wkv_ref, cos_ref, sin_ref,
                     q_ref, k_ref, v_ref):
    def rms(v, g):
        return v * lax.rsqrt(jnp.mean(v * v, axis=-1, keepdims=True) + EPS) * g

    cos, sin = cos_ref[...], sin_ref[...]

    def rope(t):
        return t * cos + pltpu.roll(t, MLA_ROPE, axis=1) * sin

    qn = rms(p_ref[:, :Q_LORA], qg_ref[...]).astype(bf16)
    kvn = rms(p_ref[:, Q_LORA:Q_LORA + KV_LORA], kvg_ref[...]).astype(bf16)
    scale = (MLA_NOPE + MLA_ROPE) ** -0.5 * math.log2(math.e)
    q = jnp.dot(qn, wq_ref[...], preferred_element_type=f32) * scale
    hw = 2 * LANES
    for h in range(MLA_HEADS):
        q_ref[:, h * hw:h * hw + LANES] = q[:, h * hw:h * hw + LANES].astype(bf16)
        q_ref[:, h * hw + LANES:(h + 1) * hw] = rope(q[:, h * hw + LANES:(h + 1) * hw]).astype(bf16)
    kv = jnp.dot(kvn, wkv_ref[...], preferred_element_type=f32)
    kpe = rope(p_ref[:, Q_LORA + KV_LORA:Q_LORA + KV_LORA + LANES]).astype(bf16)
    nk = MLA_HEADS * MLA_NOPE
    ones_col = jnp.where(lax.broadcasted_iota(jnp.int32, (kpe.shape[0], LANES), 1) == 0, 1.0, 0.0).astype(bf16)
    for h in range(MLA_HEADS):
        k_ref[:, h * hw:h * hw + LANES] = kv[:, h * MLA_NOPE:(h + 1) * MLA_NOPE].astype(bf16)
        k_ref[:, h * hw + LANES:(h + 1) * hw] = kpe
        v_ref[:, h * hw:h * hw + LANES] = kv[:, nk + h * MLA_V:nk + (h + 1) * MLA_V].astype(bf16)
        v_ref[:, h * hw + LANES:(h + 1) * hw] = ones_col


def mla_prep(proj, qn_g, kvn_g, wq, wkv, layer, cos_t, sin_t, tiles_per_batch):
    t = proj.shape[0]
    pw = proj.shape[1]
    qw = MLA_HEADS * 2 * LANES
    return pl.pallas_call(
        _mla_prep_kernel,
        out_shape=[jax.ShapeDtypeStruct((t, qw), bf16),
                   jax.ShapeDtypeStruct((t, qw), bf16),
                   jax.ShapeDtypeStruct((t, qw), bf16)],
        grid=(t // ROW_TILE,),
        in_specs=[pl.BlockSpec((ROW_TILE, pw), lambda i: (i, 0)),
                  pl.BlockSpec((1, Q_LORA), lambda i: (0, 0)),
                  pl.BlockSpec((1, KV_LORA), lambda i: (0, 0)),
                  pl.BlockSpec((None,) + wq.shape[1:], lambda i: (layer, 0, 0)),
                  pl.BlockSpec((None,) + wkv.shape[1:], lambda i: (layer, 0, 0)),
                  pl.BlockSpec((ROW_TILE, LANES), lambda i: (i % tiles_per_batch, 0)),
                  pl.BlockSpec((ROW_TILE, LANES), lambda i: (i % tiles_per_batch, 0))],
        out_specs=[pl.BlockSpec((ROW_TILE, qw), lambda i: (i, 0)),
                   pl.BlockSpec((ROW_TILE, qw), lambda i: (i, 0)),
                   pl.BlockSpec((ROW_TILE, qw), lambda i: (i, 0))],
        compiler_params=_params(("parallel",)),
        name="mla_prep",
    )(proj, qn_g.reshape(1, -1), kvn_g.reshape(1, -1), wq, wkv, cos_t, sin_t)


MLA_Q_TILE = 1024
MLA_SUB_TILE = 256


def _mla_attn_kernel(q_ref, k_ref, v_ref, o_ref):
    for r0 in range(0, q_ref.shape[0], MLA_SUB_TILE):
        rows = slice(r0, r0 + MLA_SUB_TILE)
        s = lax.dot_general(q_ref[rows, :], k_ref[...], (((1,), (1,)), ((), ())), preferred_element_type=f32)
        p = jnp.exp2((s - jnp.max(s, axis=-1, keepdims=True)).astype(bf16))
        ov = jnp.dot(p, v_ref[...], preferred_element_type=f32)
        o_ref[rows, :] = (ov[:, :MLA_V] / ov[:, MLA_V:MLA_V + 1]).astype(o_ref.dtype)


def mla_attention(q, k, v, n_batch, n_lat):
    p_len = q.shape[1]
    n_ctx = p_len - n_lat
    tq = MLA_Q_TILE if n_lat % MLA_Q_TILE == 0 else ROW_TILE
    assert n_lat % n_ctx == 0
    cb = n_lat // n_ctx
    o_lat = pl.pallas_call(
        _mla_attn_kernel,
        out_shape=jax.ShapeDtypeStruct((n_batch, n_lat, MLA_HEADS * MLA_V), bf16),
        grid=(n_batch, MLA_HEADS, n_lat // tq),
        in_specs=[pl.BlockSpec((None, tq, 2 * LANES), lambda b, h, i: (b, i, h)),
                  pl.BlockSpec((None, p_len, 2 * LANES), lambda b, h, i: (b, 0, h)),
                  pl.BlockSpec((None, p_len, 2 * LANES), lambda b, h, i: (b, 0, h))],
        out_specs=pl.BlockSpec((None, tq, MLA_V), lambda b, h, i: (b, i, h)),
        compiler_params=_params(("parallel", "parallel", "arbitrary")),
        name="mla_attention",
    )(q, k, v)
    o_ctx = pl.pallas_call(
        _mla_attn_kernel,
        out_shape=jax.ShapeDtypeStruct((n_batch, n_ctx, MLA_HEADS * MLA_V), bf16),
        grid=(n_batch, MLA_HEADS),
        in_specs=[pl.BlockSpec((None, n_ctx, 2 * LANES), lambda b, h: (b, cb, h)),
                  pl.BlockSpec((None, n_ctx, 2 * LANES), lambda b, h: (b, cb, h)),
                  pl.BlockSpec((None, n_ctx, 2 * LANES), lambda b, h: (b, cb, h))],
        out_specs=pl.BlockSpec((None, n_ctx, MLA_V), lambda b, h: (b, 0, h)),
        compiler_params=_params(("parallel", "parallel")),
        name="mla_attention_ctx",
    )(q, k, v)
    return jnp.concatenate([o_lat, o_ctx], axis=1)


NA_QROWS = 8
NA_BAND = 16


def _na_geometry(rows):
    assert rows % NA_QROWS == 0 and rows >= NA_BAND, rows
    nblk = rows // NA_QROWS
    starts = [min(max(NA_QROWS * i - WIN_R // 2, 0), rows - NA_BAND) for i in range(nblk)]
    if nblk <= 3:
        reps, pat = list(range(nblk)), list(range(nblk))
    else:
        reps = [0, 1, nblk - 1]
        pat = [0] + [1] * (nblk - 2) + [2]
        for i in range(1, nblk - 1):
            assert starts[i] == NA_QROWS * i - WIN_R // 2
    return nblk, starts, reps, pat


def na_bias_tables(rpb, rows):
    _, starts, reps, _ = _na_geometry(rows)
    nh = rpb.shape[0]
    wr = min(WIN_R, rows)
    n_off = 2 * WIN_R - 1
    qc = jnp.arange(GRID_W)
    cs = jnp.clip(qc - WIN_C // 2, 0, GRID_W - WIN_C)
    kc = jnp.arange(GRID_W)
    col_ok = (kc[None, :] >= cs[:, None]) & (kc[None, :] < cs[:, None] + WIN_C)
    col_off = jnp.clip(kc[None, :] - qc[:, None] + (WIN_C - 1), 0, 2 * WIN_C - 2)
    toe = jnp.where(col_ok[None, None], rpb.astype(f32)[:, :, col_off], NEG_BIG)
    toe = jnp.concatenate([toe, jnp.full((nh, 1, GRID_W, GRID_W), NEG_BIG, f32)], axis=1)
    sel = []
    for i in reps:
        qr = NA_QROWS * i + jnp.arange(NA_QROWS)
        kr = starts[i] + jnp.arange(NA_BAND)
        rs = jnp.clip(qr - wr // 2, 0, rows - wr)
        row_ok = (kr[None, :] >= rs[:, None]) & (kr[None, :] < rs[:, None] + wr)
        row_off = jnp.where(row_ok, kr[None, :] - qr[:, None] + (WIN_R - 1), n_off)
        sel.append(jax.nn.one_hot(row_off, n_off + 1, dtype=f32))
    sel = jnp.stack(sel, axis=0)
    tab = jnp.einsum("pakd,hdqc->phaqkc", sel, toe, precision=lax.Precision.HIGHEST)
    return tab.reshape(len(reps), nh, NA_QROWS * GRID_W, NA_BAND * GRID_W)


def _na_kernel(q_ref, k_ref, v_ref, bias_ref, o_ref, *, n_lat, starts, pat):
    scale = NA_DIM ** -0.5
    k_ctx = k_ref[n_lat:, :]
    v_ctx = v_ref[n_lat:, :]
    nt = (((1,), (1,)), ((), ()))
    qt = NA_QROWS * GRID_W
    bt = NA_BAND * GRID_W
    for i, (st, pt) in enumerate(zip(starts, pat)):
        q = q_ref[i * qt:(i + 1) * qt, :]
        kb = k_ref[st * GRID_W:st * GRID_W + bt, :]
        vb = v_ref[st * GRID_W:st * GRID_W + bt, :]
        s_loc = lax.dot_general(q, kb, nt, preferred_element_type=f32) * scale + bias_ref[pt]
        s_ctx = lax.dot_general(q, k_ctx, nt, preferred_element_type=f32) * scale
        m = jnp.maximum(jnp.max(s_loc, axis=-1, keepdims=True), jnp.max(s_ctx, axis=-1, keepdims=True))
        p_loc = jnp.exp(s_loc - m)
        p_ctx = jnp.exp(s_ctx - m)
        l = jnp.sum(p_loc, axis=-1, keepdims=True) + jnp.sum(p_ctx, axis=-1, keepdims=True)
        o = (jnp.dot(p_loc.astype(bf16), vb, preferred_element_type=f32)
             + jnp.dot(p_ctx.astype(bf16), v_ctx, preferred_element_type=f32))
        o_ref[i * qt:(i + 1) * qt, :] = (o / l).astype(o_ref.dtype)
    qz = q_ref[n_lat:, :]
    s = lax.dot_general(qz, k_ctx, nt, preferred_element_type=f32) * scale
    p = jnp.exp(s - jnp.max(s, axis=-1, keepdims=True))
    o = jnp.dot(p.astype(bf16), v_ctx, preferred_element_type=f32) / jnp.sum(p, axis=-1, keepdims=True)
    o_ref[n_lat:, :] = o.astype(o_ref.dtype)


def neighbourhood_attention(qkv, bias, n_batch, n_lat):
    p_len = qkv.shape[1]
    rows = n_lat // GRID_W
    _, starts, _, pat = _na_geometry(rows)
    npat = bias.shape[0]
    kern = functools.partial(_na_kernel, n_lat=n_lat, starts=tuple(starts), pat=tuple(pat))
    hspec = lambda off: pl.BlockSpec((None, p_len, NA_DIM), lambda h, b: (b, 0, off + h))
    return pl.pallas_call(
        kern,
        out_shape=jax.ShapeDtypeStruct((n_batch, p_len, NA_HEADS * NA_DIM), bf16),
        grid=(NA_HEADS, n_batch),
        in_specs=[hspec(0), hspec(NA_HEADS), hspec(2 * NA_HEADS),
                  pl.BlockSpec((npat, None) + bias.shape[2:], lambda h, b: (0, h, 0, 0))],
        out_specs=pl.BlockSpec((None, p_len, NA_DIM), lambda h, b: (b, 0, h)),
        compiler_params=_params(("parallel", "parallel")),
        name="neighbourhood_attention",
    )(qkv, qkv, qkv, bias)


def _segment_conv(x, w_ref, n_lat):
    p_len = x.shape[0]
    row = lax.broadcasted_iota(jnp.int32, (p_len, 1), 0)
    local = jnp.where(row < n_lat, row, row - n_lat)
    seg_len = jnp.where(row < n_lat, n_lat, p_len - n_lat)
    y = x * w_ref[CONV_PAD_L:CONV_PAD_L + 1, :]
    for j in range(CONV_W):
        off = j - CONV_PAD_L
        if off == 0:
            continue
        shifted = pltpu.roll(x, (-off) % p_len, axis=0)
        ok = (local + off >= 0) & (local + off < seg_len)
        y = y + jnp.where(ok, shifted, 0.0) * w_ref[j:j + 1, :]
    return y


SCAN_CHUNK = 256


SUBLANES = 8


def _chunk_scan(a, b, reverse, h_prev):
    n = a.shape[0]
    sub = lax.broadcasted_iota(jnp.int32, (n, 1), 0) % SUBLANES
    k = 1
    while k < SUBLANES:
        if reverse:
            a_s = pltpu.roll(a, n - k, axis=0)
            b_s = pltpu.roll(b, n - k, axis=0)
            ok = sub < SUBLANES - k
        else:
            a_s = pltpu.roll(a, k, axis=0)
            b_s = pltpu.roll(b, k, axis=0)
            ok = sub >= k
        b = b + a * jnp.where(ok, b_s, 0.0)
        a = a * jnp.where(ok, a_s, 1.0)
        k *= 2
    tiles = list(range(n // SUBLANES))
    out = [None] * len(tiles)
    for j in (tiles[::-1] if reverse else tiles):
        rows = slice(j * SUBLANES, (j + 1) * SUBLANES)
        h = b[rows] + a[rows] * h_prev
        h_prev = h[0:1] if reverse else h[SUBLANES - 1:SUBLANES]
        out[j] = h
    return jnp.concatenate(out, axis=0), h_prev


def _lru_kernel(lu_ref, ly_ref, cw_ref, cb_ref, wa_ref, ba_ref, wx_ref, bx_ref, lam_ref, o_ref,
                u_scr, a_scr, b_scr, h_scr, *, n_lat):
    p_len = lu_ref.shape[0]
    u_scr[...] = _segment_conv(lu_ref[...], cw_ref, n_lat) + cb_ref[...]
    segments = ((n_lat, p_len - n_lat), (0, n_lat))

    for d, reverse in enumerate((False, True)):
        ub = u_scr[...].astype(bf16)
        r = _sigmoid(jnp.dot(ub, wa_ref[d].astype(bf16), preferred_element_type=f32) + ba_ref[d:d + 1, :])
        g = _sigmoid(jnp.dot(ub, wx_ref[d].astype(bf16), preferred_element_type=f32) + bx_ref[d:d + 1, :])
        log_a = (-LRU_C * r) * _softplus(-lam_ref[d:d + 1, :])
        a_scr[...] = jnp.exp(log_a)
        th = jnp.tanh(log_a)
        b_scr[...] = jnp.sqrt(-2.0 * th / (1.0 - th)) * (g * u_scr[...])

        state = jnp.zeros((1, LANES), f32)
        for start, length in segments:
            n_chunks = length // SCAN_CHUNK

            def body(c, h_prev, start=start, n_chunks=n_chunks, reverse=reverse):
                ci = (n_chunks - 1 - c) if reverse else c
                rows = pl.ds(pl.multiple_of(start + ci * SCAN_CHUNK, SCAN_CHUNK), SCAN_CHUNK)
                h, h_last = _chunk_scan(a_scr[rows, :], b_scr[rows, :], reverse, h_prev)
                if d == 0:
                    h_scr[rows, :] = h
                else:
                    h_scr[rows, :] = h_scr[rows, :] + h
                return h_last

            state = lax.fori_loop(0, n_chunks, body, state)

    o_ref[...] = (h_scr[...] * _gelu_tanh(ly_ref[...])).astype(o_ref.dtype)


def rglru(luy, conv_w, conv_b, wa, ba, wx, bx, lam, n_batch, n_lat):
    p_len = luy.shape[1]
    nblk = LRU_W // LANES
    assert LRU_W // LRU_BLOCKS == LANES and n_lat % SCAN_CHUNK == 0 and (p_len - n_lat) % SCAN_CHUNK == 0
    vec = lambda rows: pl.BlockSpec((rows, LANES), lambda b, j: (0, j))
    return pl.pallas_call(
        functools.partial(_lru_kernel, n_lat=n_lat),
        out_shape=jax.ShapeDtypeStruct((n_batch, p_len, LRU_W), bf16),
        grid=(n_batch, nblk),
        in_specs=[pl.BlockSpec((None, p_len, LANES), lambda b, j: (b, 0, j)),
                  pl.BlockSpec((None, p_len, LANES), lambda b, j: (b, 0, nblk + j)),
                  vec(CONV_W), vec(1),
                  pl.BlockSpec((2, None, LANES, LANES), lambda b, j: (0, j, 0, 0)), vec(2),
                  pl.BlockSpec((2, None, LANES, LANES), lambda b, j: (0, j, 0, 0)), vec(2),
                  vec(2)],
        out_specs=pl.BlockSpec((None, p_len, LANES), lambda b, j: (b, 0, j)),
        scratch_shapes=[pltpu.VMEM((p_len, LANES), f32) for _ in range(4)],
        compiler_params=_params(("parallel", "parallel")),
        name="rglru",
    )(luy, luy, conv_w, conv_b.reshape(1, LRU_W), wa, ba, wx, bx, lam)


def _dn_pre_kernel(x_ref, w_ref, o_ref, *, n_lat):
    j = pl.program_id(1)
    u = _silu(_segment_conv(x_ref[...], w_ref, n_lat))
    nrm = u * lax.rsqrt(jnp.sum(u * u, axis=-1, keepdims=True) + EPS)
    qk_scale = jnp.where(j < DN_HEADS, DN_DK ** -0.5, 1.0)
    o_ref[...] = jnp.where(j < 2 * DN_HEADS, nrm * qk_scale, u).astype(o_ref.dtype)


def deltanet_pre(dqkv, conv_w, n_batch, n_lat, col_blocks):
    p_len = dqkv.shape[1]
    return pl.pallas_call(
        functools.partial(_dn_pre_kernel, n_lat=n_lat),
        out_shape=jax.ShapeDtypeStruct((n_batch, p_len, col_blocks * LANES), bf16),
        grid=(n_batch, col_blocks),
        in_specs=[pl.BlockSpec((None, p_len, LANES), lambda b, j: (b, 0, j)),
                  pl.BlockSpec((CONV_W, LANES), lambda b, j: (0, j))],
        out_specs=pl.BlockSpec((None, p_len, LANES), lambda b, j: (b, 0, j)),
        compiler_params=_params(("parallel", "parallel")),
        name="deltanet_pre",
    )(dqkv, conv_w)


def _dn_gate_kernel(ab_ref, alog_ref, dt_ref, o_ref):
    nh2 = 2 * DN_HEADS
    a_raw = ab_ref[:, :nh2]
    b_raw = ab_ref[:, nh2:2 * nh2]
    g = -jnp.exp(alog_ref[...]) * _softplus(a_raw + dt_ref[...])
    beta = _sigmoid(b_raw)
    n = g.shape[0]
    r = lax.broadcasted_iota(jnp.int32, (n, n), 0)
    c = lax.broadcasted_iota(jnp.int32, (n, n), 1)
    same = (r // DN_CHUNK) == (c // DN_CHUNK)
    hi = lax.Precision.HIGHEST
    pre = jnp.dot(jnp.where(same & (c <= r), 1.0, 0.0), g, preferred_element_type=f32, precision=hi)
    suf = jnp.dot(jnp.where(same & (c >= r), 1.0, 0.0), g, preferred_element_type=f32, precision=hi)
    tot = jnp.dot(jnp.where(same, 1.0, 0.0), g, preferred_element_type=f32, precision=hi)
    pad = jnp.zeros((n, LANES - 3 * nh2), f32)
    o_ref[...] = jnp.concatenate([pre[:, :DN_HEADS], suf[:, DN_HEADS:], beta, tot, pad], axis=-1)


def deltanet_gates(ab, col_block, a_log, dt_bias):
    t = ab.shape[0]
    nh2 = 2 * DN_HEADS
    return pl.pallas_call(
        _dn_gate_kernel,
        out_shape=jax.ShapeDtypeStruct((t, LANES), f32),
        grid=(t // ROW_TILE,),
        in_specs=[pl.BlockSpec((ROW_TILE, LANES), lambda i: (i, col_block)),
                  pl.BlockSpec((1, nh2), lambda i: (0, 0)),
                  pl.BlockSpec((1, nh2), lambda i: (0, 0))],
        out_specs=pl.BlockSpec((ROW_TILE, LANES), lambda i: (i, 0)),
        compiler_params=_params(("parallel",)),
        name="deltanet_gates",
    )(ab, a_log.reshape(1, nh2), dt_bias.reshape(1, nh2))


def _tri_inverse_minus_eye(low):
    c = low.shape[-1]
    bmm = functools.partial(jnp.einsum, "gij,gjk->gik", preferred_element_type=f32)
    x = -low
    acc = x
    k = 2
    while k < c:
        xb = x.astype(bf16)
        x = bmm(xb, xb)
        acc = acc + x + bmm(x.astype(bf16), acc.astype(bf16))
        k *= 2
    return acc


DN_GROUP_MAX = 17


def _dn_kernel(q_ref, k_ref, v_ref, z_ref, gcol_ref, grow_ref, ng_ref, o_ref,
               lhs_scr, inc_scr, gend_scr, acc_scr, *, n_lat, grp):
    p_len = q_ref.shape[0]
    cs = DN_CHUNK
    n_chunks = p_len // cs
    nc_lat = n_lat // cs
    nc_ctx = n_chunks - nc_lat
    nb = 2 * grp
    ri = lax.broadcasted_iota(jnp.int32, (nb, cs, cs), 1)
    ci = lax.broadcasted_iota(jnp.int32, (nb, cs, cs), 2)
    rev = lax.broadcasted_iota(jnp.int32, (nb, cs, cs), 0) >= grp
    ahead = jnp.where(rev, ri - ci, ci - ri)
    incl = ahead <= 0
    strict = ahead < 0

    def phase1(g, carry):
        c0 = g * grp
        rows = pl.ds(pl.multiple_of(c0 * cs, cs * grp), cs * grp)

        def both(x):
            x = x.astype(f32).reshape(grp, cs, x.shape[-1])
            return jnp.concatenate([x, x], axis=0)

        def per_dir(col):
            return jnp.concatenate([gcol_ref[rows, col + d:col + d + 1].reshape(grp, cs, 1) for d in range(2)], axis=0)

        q, k, v = both(q_ref[rows, :]), both(k_ref[rows, :]), both(v_ref[rows, :])
        gc, beta, gtot = per_dir(0), per_dir(2), per_dir(4)
        gr = jnp.concatenate([grow_ref[d, pl.ds(c0, grp), :] for d in range(2)], axis=0)[:, None, :]
        decay = jnp.where(incl, jnp.exp(jnp.where(incl, gc - gr, 0.0)), 0.0)
        e_gc = jnp.exp(gc)
        kb = k * beta
        gram = jnp.einsum("gik,gjk->gij", jnp.concatenate([kb, q], axis=1).astype(bf16), k.astype(bf16),
                          preferred_element_type=f32)
        t_m1 = _tri_inverse_minus_eye(jnp.where(strict, gram[:, :cs] * decay, 0.0))
        rhs = jnp.concatenate([v * beta, kb * e_gc], axis=2)
        w = rhs + jnp.einsum("gij,gjk->gik", t_m1.astype(bf16), rhs.astype(bf16), preferred_element_type=f32)
        k_dec = (k * jnp.exp(gtot - gc)).astype(bf16)
        a_in = (gram[:, cs:] * decay).astype(bf16)
        w_hi = w.astype(bf16)
        wv_lo = (w[:, :, :DN_DV] - w_hi[:, :, :DN_DV].astype(f32)).astype(bf16)
        kdt_w = jnp.einsum("gck,gcn->gkn", k_dec, w_hi, preferred_element_type=f32)
        s_inc = kdt_w[:, :, :DN_DV] + jnp.einsum("gck,gcn->gkn", k_dec, wv_lo, preferred_element_type=f32)
        ain_w = jnp.einsum("gij,gjn->gin", a_in, w_hi, preferred_element_type=f32)
        lhs = jnp.concatenate([-kdt_w[:, :, DN_DV:], q * e_gc - ain_w[:, :, DN_DV:]], axis=1).astype(bf16)
        g_end = jnp.broadcast_to(jnp.exp(gtot[:, 0:1, :]), (nb, 1, DN_DV))
        o_local = ain_w[:, :, :DN_DV]
        acc_scr[rows, :] = (o_local[:grp] + o_local[grp:]).reshape(grp * cs, DN_DV)
        for d in range(2):
            sl = slice(d * grp, (d + 1) * grp)
            dst = pl.ds(c0, grp)
            lhs_scr[d, dst] = lhs[sl]
            inc_scr[d, dst] = s_inc[sl]
            gend_scr[d, dst] = g_end[sl]
        return carry

    lax.fori_loop(0, n_chunks // grp, phase1, 0)

    def phase2(t, states):
        in_ctx = t < nc_ctx
        chunk = (jnp.where(in_ctx, nc_lat + t, t - nc_ctx),
                 jnp.where(in_ctx, n_chunks - 1 - t, nc_lat - 1 - (t - nc_ctx)))
        new_states = []
        for d in range(2):
            c = chunk[d]
            s = states[d]
            a = jnp.dot(lhs_scr[d, c], s.astype(bf16), preferred_element_type=f32)
            rows = pl.ds(pl.multiple_of(c * cs, cs), cs)
            acc_scr[rows, :] = acc_scr[rows, :] + a[DN_DK:]
            new_states.append(s * gend_scr[d, c] + inc_scr[d, c] + a[:DN_DK])
        return tuple(new_states)

    zero = jnp.zeros((DN_DK, DN_DV), f32)
    lax.fori_loop(0, n_chunks, phase2, (zero, zero))

    o = acc_scr[...]
    y = o * lax.rsqrt(jnp.mean(o * o, axis=-1, keepdims=True) + EPS) * ng_ref[...]
    o_ref[...] = (y * _silu(z_ref[...])).astype(o_ref.dtype)


def deltanet(qkv, dz_src, z_col0, gcol, grow, norm_g, n_batch, n_lat):
    p_len = qkv.shape[1]
    n_chunks = p_len // DN_CHUNK
    assert n_lat % DN_CHUNK == 0 and p_len % DN_CHUNK == 0
    grp = max(g for g in range(1, DN_GROUP_MAX + 1) if n_chunks % g == 0)
    hspec = lambda off: pl.BlockSpec((None, p_len, LANES), lambda b, h: (b, 0, off + h))
    return pl.pallas_call(
        functools.partial(_dn_kernel, n_lat=n_lat, grp=grp),
        out_shape=jax.ShapeDtypeStruct((n_batch, p_len, DN_HEADS * DN_DV), bf16),
        grid=(n_batch, DN_HEADS),
        in_specs=[hspec(0), hspec(DN_HEADS), hspec(2 * DN_HEADS),
                  pl.BlockSpec((None, p_len, LANES), lambda b, h: (b, 0, z_col0 + h)),
                  pl.BlockSpec((None, None, p_len, 8), lambda b, h: (b, h, 0, 0)),
                  pl.BlockSpec((None, None, 2, p_len // DN_CHUNK, DN_CHUNK), lambda b, h: (b, h, 0, 0, 0)),
                  pl.BlockSpec((1, DN_DV), lambda b, h: (0, 0))],
        out_specs=pl.BlockSpec((None, p_len, LANES), lambda b, h: (b, 0, h)),
        scratch_shapes=[pltpu.VMEM((2, n_chunks, DN_DK + DN_CHUNK, DN_DK), bf16),
                        pltpu.VMEM((2, n_chunks, DN_DK, DN_DV), f32),
                        pltpu.VMEM((2, n_chunks, 1, DN_DV), f32),
                        pltpu.VMEM((p_len, DN_DV), f32)],
        compiler_params=_params(("parallel", "parallel")),
        name="deltanet",
    )(qkv, qkv, qkv, dz_src, gcol, grow, norm_g.reshape(1, DN_DV))


def _merge_kernel(*refs):
    o_refs, h_ref, wg_refs, wb_ref, out_ref = (refs[:N_BRANCH], refs[N_BRANCH], refs[N_BRANCH + 1:2 * N_BRANCH + 1],
                                               refs[-2], refs[-1])
    h = h_ref[...]
    acc = None
    for i in range(N_BRANCH):
        y = jnp.dot(o_refs[i][...], wb_ref[i], preferred_element_type=f32)
        y = _sigmoid(jnp.dot(h, wg_refs[i][...], preferred_element_type=f32)) * y
        acc = y if acc is None else acc + y
    out_ref[...] = acc.astype(out_ref.dtype)


def merge_branches(branches, h, w_all, off_gate, w_branch, layer):
    t, bw = branches[0].shape
    d = w_branch.shape[3]
    tm, tn = ROW_TILE * 2, min(512, d)
    while t % tm:
        tm //= 2
    nj = d // tn
    assert off_gate % tn == 0 and d % tn == 0
    bspec = pl.BlockSpec((tm, bw), lambda j, i: (i, 0))
    gspecs = [pl.BlockSpec((None, d, tn),
                           functools.partial(lambda j, i, cb: (layer, 0, cb + j), cb=(off_gate + br * d) // tn))
              for br in range(N_BRANCH)]
    return pl.pallas_call(
        _merge_kernel,
        out_shape=jax.ShapeDtypeStruct((t, d), bf16),
        grid=(nj, t // tm),
        in_specs=([bspec] * N_BRANCH + [pl.BlockSpec((tm, d), lambda j, i: (i, 0))] + gspecs
                  + [pl.BlockSpec((None, N_BRANCH, bw, tn), lambda j, i: (layer, 0, 0, j))]),
        out_specs=pl.BlockSpec((tm, tn), lambda j, i: (i, j)),
        compiler_params=_params(("parallel", "parallel")),
        name="merge_branches",
    )(*branches, h, *([w_all] * N_BRANCH), w_branch)


def _out_proj_kernel(m_ref, w_ref, x_ref, mod_ref, o_ref, *, gate_row):
    y = jnp.dot(m_ref[...], w_ref[...], preferred_element_type=f32)
    o_ref[...] = x_ref[...] + mod_ref[gate_row:gate_row + 1, :] * y


def out_proj_residual(merged, w_out, layer, x, mod, gate_row, geom):
    t, d = x.shape
    n_batch, tpb, lat = geom
    mrow = functools.partial(_mod_row, tiles_per_batch=tpb, lat_tiles=lat, n_batch=n_batch)
    return pl.pallas_call(
        functools.partial(_out_proj_kernel, gate_row=gate_row),
        out_shape=jax.ShapeDtypeStruct((t, d), f32),
        grid=(t // ROW_TILE,),
        in_specs=[pl.BlockSpec((ROW_TILE, d), lambda i: (i, 0)),
                  pl.BlockSpec((None, d, d), lambda i: (layer, 0, 0)),
                  pl.BlockSpec((ROW_TILE, d), lambda i: (i, 0)),
                  pl.BlockSpec((None, N_MOD, d), lambda i: (mrow(i), 0, 0))],
        out_specs=pl.BlockSpec((ROW_TILE, d), lambda i: (i, 0)),
        compiler_params=_params(("parallel",)),
        name="out_proj_residual",
    )(merged, w_out, x, mod)


def _route_kernel(h_ref, rw_ref, rb_ref, sel_ref, cnt_ref, carry):
    i = pl.program_id(0)

    @pl.when(i == 0)
    def _():
        carry[...] = jnp.zeros_like(carry)

    logits = jnp.dot(h_ref[...], rw_ref[...], preferred_element_type=f32,
                     precision=lax.Precision.HIGHEST) + rb_ref[...]
    n = logits.shape[0]
    lane = lax.broadcasted_iota(jnp.int32, (n, LANES), 1).astype(f32)
    work = logits
    vals, idxs, hots = [], [], []
    for _ in range(TOP_K):
        m = jnp.max(work, axis=-1, keepdims=True)
        idx = jnp.min(jnp.where(work == m, lane, float(LANES)), axis=-1, keepdims=True)
        hot = lane == idx
        vals.append(m)
        idxs.append(idx)
        hots.append(hot)
        work = jnp.where(hot, NEG_BIG * 2.0, work)
    exps = [jnp.exp(v - vals[0]) for v in vals]
    den = exps[0]
    for e in exps[1:]:
        den = den + e
    onehot = jnp.zeros((n, LANES), f32)
    for hot in hots:
        onehot = onehot + jnp.where(hot, 1.0, 0.0)
    r = lax.broadcasted_iota(jnp.int32, (n, n), 0)
    c = lax.broadcasted_iota(jnp.int32, (n, n), 1)
    before = jnp.dot(jnp.where(c < r, 1.0, 0.0).astype(bf16), onehot.astype(bf16),
                     preferred_element_type=f32) + carry[...]
    out = jnp.zeros((n, LANES), f32)
    for kk in range(TOP_K):
        rank = jnp.sum(jnp.where(hots[kk], before, 0.0), axis=-1, keepdims=True)
        out = jnp.where(lane == float(kk), idxs[kk], out)
        out = jnp.where(lane == float(TOP_K + kk), rank, out)
        out = jnp.where(lane == float(2 * TOP_K + kk), exps[kk] / den, out)
    sel_ref[...] = out
    carry[...] = carry[...] + jnp.sum(onehot, axis=0, keepdims=True)
    cnt_ref[...] = carry[...]


def moe_route(h, router_w, router_b):
    t, d = h.shape
    e = router_w.shape[1]
    rw = jnp.zeros((d, LANES), f32).at[:, :e].set(router_w)
    rb = jnp.full((1, LANES), NEG_BIG, f32).at[0, :e].set(router_b)
    return pl.pallas_call(
        _route_kernel,
        out_shape=[jax.ShapeDtypeStruct((t, LANES), f32), jax.ShapeDtypeStruct((1, LANES), f32)],
        grid=(t // ROW_TILE,),
        in_specs=[pl.BlockSpec((ROW_TILE, d), lambda i: (i, 0)),
                  pl.BlockSpec((d, LANES), lambda i: (0, 0)),
                  pl.BlockSpec((1, LANES), lambda i: (0, 0))],
        out_specs=[pl.BlockSpec((ROW_TILE, LANES), lambda i: (i, 0)),
                   pl.BlockSpec((1, LANES), lambda i: (0, 0))],
        scratch_shapes=[pltpu.VMEM((1, LANES), f32)],
        compiler_params=_params(("arbitrary",)),
        name="moe_route",
    )(h, rw, rb)


def _dispatch_kernel(pos_hbm, x_ref, init_ref, xs_hbm, pos_smem, sem_pos, sem_rows):
    del init_ref
    i = pl.program_id(0)
    n_sel = ROW_TILE * TOP_K
    cp = pltpu.make_async_copy(pos_hbm.at[i], pos_smem, sem_pos)
    cp.start()
    cp.wait()

    def issue(r, c):
        src = x_ref.at[pl.ds(r, 1)]
        for kk in range(TOP_K):
            pltpu.make_async_copy(src, xs_hbm.at[pl.ds(pos_smem[r * TOP_K + kk], 1)], sem_rows).start()
        return c

    lax.fori_loop(0, ROW_TILE, issue, 0, unroll=4)
    pltpu.make_async_copy(xs_hbm.at[pl.ds(0, n_sel)], xs_hbm.at[pl.ds(0, n_sel)], sem_rows).wait()


def moe_dispatch(h, pos, n_slots):
    t, d = h.shape
    tiles = t // ROW_TILE
    init = jnp.zeros((n_slots, d), h.dtype)
    return pl.pallas_call(
        _dispatch_kernel,
        out_shape=jax.ShapeDtypeStruct((n_slots, d), h.dtype),
        grid=(tiles,),
        in_specs=[pl.BlockSpec(memory_space=pl.ANY),
                  pl.BlockSpec((ROW_TILE, d), lambda i: (i, 0)),
                  pl.BlockSpec(memory_space=pl.ANY)],
        out_specs=pl.BlockSpec(memory_space=pl.ANY),
        scratch_shapes=[pltpu.SMEM((ROW_TILE * TOP_K,), jnp.int32),
                        pltpu.SemaphoreType.DMA, pltpu.SemaphoreType.DMA],
        input_output_aliases={2: 0},
        compiler_params=_params(("arbitrary",)),
        name="moe_dispatch",
    )(pos.reshape(tiles, ROW_TILE * TOP_K), h, init)


def _expert_kernel(te_ref, xs_ref, w1_ref, b1_ref, sel_ref, w2_ref, b2_ref, ys_ref):
    del te_ref
    x = jnp.concatenate(_unpack_halves(xs_ref[...]), axis=1).astype(bf16)
    u = jnp.dot(x, w1_ref[...], preferred_element_type=f32) + b1_ref[...]
    n = u.shape[1]
    lin = pltpu.roll(u, n - 1, axis=1)
    glu = jnp.minimum(u, SWIGLU_LIMIT)
    lin = jnp.clip(lin, -SWIGLU_LIMIT, SWIGLU_LIMIT)
    act = glu * _sigmoid(SWIGLU_ALPHA * glu) * (lin + 1.0)
    even = lax.broadcasted_iota(jnp.int32, u.shape, 1) % 2 == 0
    act = jnp.where(even, act, 0.0).astype(bf16)
    act = jnp.dot(act, sel_ref[...], preferred_element_type=f32).astype(bf16)
    ys_ref[...] = _pack_halves(jnp.dot(act, w2_ref[...], preferred_element_type=f32) + b2_ref[...])


EXPERT_TILE = 512


def moe_experts(xs, tile_expert, layer, w1, b1, w2, b2):
    s, dh = xs.shape
    d = 2 * dh
    ff2 = w1.shape[3]
    ff = ff2 // 2
    sel = (jnp.arange(ff2)[:, None] == 2 * jnp.arange(ff)[None, :]).astype(bf16)
    grid_spec = pltpu.PrefetchScalarGridSpec(
        num_scalar_prefetch=1,
        grid=(s // EXPERT_TILE,),
        in_specs=[pl.BlockSpec((EXPERT_TILE, dh), lambda i, te: (i, 0)),
                  pl.BlockSpec((None, None, d, ff2), lambda i, te: (layer, te[i], 0, 0)),
                  pl.BlockSpec((None, None, 1, ff2), lambda i, te: (layer, te[i], 0, 0)),
                  pl.BlockSpec((ff2, ff), lambda i, te: (0, 0)),
                  pl.BlockSpec((None, None, ff, d), lambda i, te: (layer, te[i], 0, 0)),
                  pl.BlockSpec((None, None, 1, d), lambda i, te: (layer, te[i], 0, 0))],
        out_specs=pl.BlockSpec((EXPERT_TILE, dh), lambda i, te: (i, 0)),
    )
    return pl.pallas_call(
        _expert_kernel,
        out_shape=jax.ShapeDtypeStruct((s, dh), jnp.uint32),
        grid_spec=grid_spec,
        compiler_params=_params(("arbitrary",)),
        name="moe_experts",
    )(tile_expert, xs, w1, b1, sel, w2, b2)


def _combine_kernel(pos_hbm, ys_hbm, x_ref, wt_ref, mod_ref, o_ref, buf, pos_smem, sem_pos, sem_rows, *, gate_row):
    i = pl.program_id(0)
    n_sel = ROW_TILE * TOP_K
    cp = pltpu.make_async_copy(pos_hbm.at[i], pos_smem, sem_pos)
    cp.start()
    cp.wait()

    def issue(r, c):
        for kk in range(TOP_K):
            pltpu.make_async_copy(ys_hbm.at[pl.ds(pos_smem[r * TOP_K + kk], 1)],
                                  buf.at[kk, pl.ds(r, 1)], sem_rows).start()
        return c

    lax.fori_loop(0, ROW_TILE, issue, 0, unroll=4)
    pltpu.make_async_copy(ys_hbm.at[pl.ds(0, n_sel)], ys_hbm.at[pl.ds(0, n_sel)], sem_rows).wait()
    m = buf.shape[2]
    acc_lo = acc_hi = None
    for kk in range(TOP_K):
        wk = wt_ref[:, 2 * TOP_K + kk:2 * TOP_K + kk + 1]
        lo, hi = _unpack_halves(buf[kk])
        acc_lo = wk * lo if acc_lo is None else acc_lo + wk * lo
        acc_hi = wk * hi if acc_hi is None else acc_hi + wk * hi
    o_ref[:, :m] = x_ref[:, :m] + mod_ref[gate_row:gate_row + 1, :m] * acc_lo
    o_ref[:, m:] = x_ref[:, m:] + mod_ref[gate_row:gate_row + 1, m:] * acc_hi


def moe_combine(ys, pos, sel, x, mod, gate_row, geom):
    t, d = x.shape
    tiles = t // ROW_TILE
    n_batch, tpb, lat = geom
    mrow = functools.partial(_mod_row, tiles_per_batch=tpb, lat_tiles=lat, n_batch=n_batch)
    return pl.pallas_call(
        functools.partial(_combine_kernel, gate_row=gate_row),
        out_shape=jax.ShapeDtypeStruct((t, d), f32),
        grid=(tiles,),
        in_specs=[pl.BlockSpec(memory_space=pl.ANY),
                  pl.BlockSpec(memory_space=pl.ANY),
                  pl.BlockSpec((ROW_TILE, d), lambda i: (i, 0)),
                  pl.BlockSpec((ROW_TILE, LANES), lambda i: (i, 0)),
                  pl.BlockSpec((None, N_MOD, d), lambda i: (mrow(i), 0, 0))],
        out_specs=pl.BlockSpec((ROW_TILE, d), lambda i: (i, 0)),
        scratch_shapes=[pltpu.VMEM((TOP_K, ROW_TILE, d // 2), jnp.uint32),
                        pltpu.SMEM((ROW_TILE * TOP_K,), jnp.int32),
                        pltpu.SemaphoreType.DMA, pltpu.SemaphoreType.DMA],
        compiler_params=_params(("arbitrary",)),
        name="moe_combine",
    )(pos.reshape(tiles, ROW_TILE * TOP_K), ys, x, sel, mod)


def moe_ffn_residual(h, h_packed, x, mod, gate_row, geom, router_w, router_b, layer, w1, b1, w2, b2):
    t, d = h.shape
    n_exp = router_w.shape[1]
    sel, counts = moe_route(h, router_w, router_b)
    eidx = sel[:, :TOP_K].astype(jnp.int32)
    rank = sel[:, TOP_K:2 * TOP_K].astype(jnp.int32)
    counts = counts[0, :n_exp].astype(jnp.int32)
    padded = ((counts + EXPERT_TILE - 1) // EXPERT_TILE) * EXPERT_TILE
    ends = jnp.cumsum(padded)
    starts = ends - padded
    pos = starts[eidx] + rank
    assert (t * TOP_K) % EXPERT_TILE == 0
    n_tiles = (t * TOP_K) // EXPERT_TILE + n_exp
    tile_start = jnp.arange(n_tiles, dtype=jnp.int32) * EXPERT_TILE
    tile_expert = jnp.minimum(jnp.sum((ends[None, :] <= tile_start[:, None]).astype(jnp.int32), axis=1), n_exp - 1)
    xs = moe_dispatch(h_packed, pos, n_tiles * EXPERT_TILE)
    ys = moe_experts(xs, tile_expert, layer, w1, b1, w2, b2)
    return moe_combine(ys, pos, sel, x, mod, gate_row, geom)


def _rope_perm(w):
    lead = w.shape[:-1]
    nf = MLA_ROPE // 4
    wr = w.reshape(*lead, 2, 2, nf)
    return jnp.stack([-wr[..., 1, :], wr[..., 0, :]], axis=-2).reshape(*lead, MLA_ROPE)


NA_COLS = 3 * NA_HEADS * NA_DIM
LRU_COLS = 2 * LRU_W
DN_COLS = DN_HEADS * (2 * DN_DK + DN_DV) + DN_HEADS * DN_DV
MLA_COLS = Q_LORA + KV_LORA + 2 * LANES


def regroup_w_in(w_in, d_model):
    sizes = (Q_LORA, KV_LORA, MLA_ROPE, NA_HEADS * NA_DIM, NA_HEADS * NA_DIM, NA_HEADS * NA_DIM, LRU_W, LRU_W,
             DN_HEADS * (2 * DN_DK + DN_DV), DN_HEADS * DN_DV, 2 * DN_HEADS, 2 * DN_HEADS, N_BRANCH * d_model)
    parts, s = [], 0
    for n in sizes:
        parts.append(w_in[:, :, s:s + n])
        s += n
    assert s == w_in.shape[2]
    (w_qc, w_kvc, w_kpe, w_naq, w_nak, w_nav, w_lu, w_ly, w_dqkv, w_dz, w_da, w_db, w_gt) = parts
    pad = jnp.zeros(w_in.shape[:2] + (LANES - 4 * DN_HEADS,), w_in.dtype)
    w = jnp.concatenate([w_naq, w_nak, w_nav, w_lu, w_ly, w_dqkv, w_dz, w_gt,
                         w_qc, w_kvc, w_kpe, _rope_perm(w_kpe), w_da, w_db, pad], axis=2).astype(bf16)
    off_na = 0
    off_lru = off_na + NA_COLS
    off_dn = off_lru + LRU_COLS
    off_gate = off_dn + DN_COLS
    off_mla = off_gate + N_BRANCH * d_model
    assert w.shape[2] == off_mla + MLA_COLS
    return w, (off_na, off_lru, off_dn, off_gate, off_mla)


def rope_tables(n_lat, p_len):
    t = jnp.arange(n_lat)
    row = (t // GRID_W).astype(f32)
    col = (t % GRID_W).astype(f32)
    nf = MLA_ROPE // 4
    inv_freq = ROPE_BASE ** (-jnp.arange(nf, dtype=f32) / nf)
    ang = jnp.concatenate([row[:, None] * inv_freq] * 2 + [col[:, None] * inv_freq] * 2, axis=1)
    zeros = jnp.zeros((n_lat, LANES - MLA_ROPE), f32)
    cos_t = jnp.concatenate([jnp.cos(ang), zeros], axis=1)
    sin_t = jnp.concatenate([jnp.sin(ang), zeros], axis=1)
    n_ctx = p_len - n_lat
    ctx_cos = jnp.concatenate([jnp.ones((n_ctx, MLA_ROPE), f32), jnp.zeros((n_ctx, LANES - MLA_ROPE), f32)], axis=1)
    return (jnp.concatenate([cos_t, ctx_cos], axis=0),
            jnp.concatenate([sin_t, jnp.zeros((n_ctx, LANES), f32)], axis=0))


def prepare_weights(w_in, mla_wq_up, mla_wkv_up, w_branch, w_out, exp_w1, exp_b1, exp_w2, exp_b2, d_model):
    depth = w_in.shape[0]
    w_all, offs = regroup_w_in(w_in, d_model)
    wq = mla_wq_up.reshape(depth, Q_LORA, MLA_HEADS, MLA_NOPE + MLA_ROPE)
    wq = jnp.concatenate([wq, _rope_perm(wq[..., MLA_NOPE:])], axis=-1).reshape(depth, Q_LORA, -1).astype(bf16)
    wkv = mla_wkv_up.reshape(depth, KV_LORA, MLA_HEADS, MLA_NOPE + MLA_V)
    wkv = jnp.concatenate([wkv[..., :MLA_NOPE].reshape(depth, KV_LORA, -1),
                           wkv[..., MLA_NOPE:].reshape(depth, KV_LORA, -1)], axis=2).astype(bf16)
    return dict(w_all=w_all, offs=offs, wq=wq, wkv=wkv, w_branch=w_branch.astype(bf16), w_out=w_out.astype(bf16),
                exp_w1=exp_w1.astype(bf16), exp_b1=exp_b1[:, :, None, :], exp_w2=exp_w2.astype(bf16),
                exp_b2=exp_b2[:, :, None, :])


def token_mixer_branches(h, l, wts, geom, n_lat, cos_t, sin_t, na_bias, mla_qn_g, mla_kvn_g, lru_conv_w, lru_conv_b,
                         lru_wa, lru_ba, lru_wx, lru_bx, lru_lam, dn_conv_w, dn_a_log, dn_dt_bias, dn_norm_g):
    n_batch, tpb, lat_tiles = geom
    t, d = h.shape
    p_len = t // n_batch
    w_all = wts["w_all"]
    off_na, off_lru, off_dn, off_gate, off_mla = wts["offs"]

    p_mla = matmul(h, w_all, l, off_mla, MLA_COLS, f32, 1024, 256, "proj_mla")
    q, k, v = mla_prep(p_mla, mla_qn_g, mla_kvn_g, wts["wq"], wts["wkv"], l, cos_t, sin_t, tpb)
    o_a = mla_attention(q.reshape(n_batch, p_len, -1), k.reshape(n_batch, p_len, -1),
                        v.reshape(n_batch, p_len, -1), n_batch, n_lat)

    p_na = matmul(h, w_all, l, off_na, NA_COLS, bf16, 1024, 512, "proj_na")
    o_b = neighbourhood_attention(p_na.reshape(n_batch, p_len, -1), na_bias, n_batch, n_lat)

    p_lru = matmul(h, w_all, l, off_lru, LRU_COLS, f32, 1024, 512, "proj_lru")
    o_c = rglru(p_lru.reshape(n_batch, p_len, -1), lru_conv_w, lru_conv_b, lru_wa, lru_ba, lru_wx, lru_bx, lru_lam,
                n_batch, n_lat)

    p_dn = matmul(h, w_all, l, off_dn, DN_COLS, f32, 1024, 512, "proj_dn").reshape(n_batch, p_len, -1)
    qkv_dn = deltanet_pre(p_dn, dn_conv_w, n_batch, n_lat, 3 * DN_HEADS)
    gates = deltanet_gates(p_mla, (MLA_COLS - LANES) // LANES, dn_a_log, dn_dt_bias).reshape(n_batch, p_len, LANES)
    nh = DN_HEADS

    def per_head(cols):
        return jnp.transpose(cols.reshape(n_batch, p_len, 2, nh), (0, 3, 1, 2))

    gc_cols = per_head(gates[..., :2 * nh])
    gcol = jnp.concatenate([gc_cols, per_head(gates[..., 2 * nh:4 * nh]), per_head(gates[..., 4 * nh:6 * nh]),
                            jnp.zeros((n_batch, nh, p_len, 2), f32)], axis=-1)
    grow = jnp.transpose(gc_cols, (0, 1, 3, 2)).reshape(n_batch, nh, 2, p_len // DN_CHUNK, DN_CHUNK)
    o_d = deltanet(qkv_dn, p_dn, 3 * DN_HEADS, gcol, grow, dn_norm_g, n_batch, n_lat)

    return o_a, o_b, o_c, o_d


def kernel(x, c, ctx, c_ctx, ada_w, ada_b, norm_mix_g, norm_ffn_g, w_in, mla_qn_g, mla_wq_up, mla_kvn_g, mla_wkv_up, na_rpb, lru_conv_w, lru_conv_b, lru_wa, lru_ba, lru_wx, lru_bx, lru_lam, dn_conv_w, dn_a_log, dn_dt_bias, dn_norm_g, w_branch, w_out, router_w, router_b, exp_w1, exp_b1, exp_w2, exp_b2, final_g):
    n_batch, n_lat, d = x.shape
    n_ctx = ctx.shape[1]
    depth = ada_w.shape[0]
    p_len = n_lat + n_ctx
    t = n_batch * p_len
    assert n_lat % ROW_TILE == 0 and n_ctx % ROW_TILE == 0 and n_lat % GRID_W == 0
    tpb, lat_tiles = p_len // ROW_TILE, n_lat // ROW_TILE
    geom = (n_batch, tpb, lat_tiles)

    stream = jnp.concatenate([x, ctx], axis=1).reshape(t, d)
    pad_rows = (-(n_batch + 1)) % 8
    cond = jnp.concatenate([c, c_ctx[None, :], jnp.zeros((pad_rows, d), f32)], axis=0)
    mods = ada_modulation(cond, ada_w, ada_b)[:, :n_batch + 1].reshape(depth, n_batch + 1, N_MOD, d)
    cos_t, sin_t = rope_tables(n_lat, p_len)

    wts = prepare_weights(w_in, mla_wq_up, mla_wkv_up, w_branch, w_out, exp_w1, exp_b1, exp_w2, exp_b2, d)
    bias_tabs = [na_bias_tables(na_rpb[l], n_lat // GRID_W) for l in range(depth)]

    for l in range(depth):
        mod = mods[l]
        (h,) = norm_modulate(stream, norm_mix_g[l], mod, 0, geom, (bf16,))
        o_a, o_b, o_c, o_d = token_mixer_branches(
            h, l, wts, geom, n_lat, cos_t, sin_t, bias_tabs[l], mla_qn_g[l], mla_kvn_g[l], lru_conv_w[l],
            lru_conv_b[l], lru_wa[l], lru_ba[l], lru_wx[l], lru_bx[l], lru_lam[l], dn_conv_w[l], dn_a_log[l],
            dn_dt_bias[l], dn_norm_g[l])

        merged = merge_branches((o_a.reshape(t, -1), o_b.reshape(t, -1), o_c.reshape(t, -1), o_d.reshape(t, -1)),
                                h, wts["w_all"], wts["offs"][3], wts["w_branch"], l)
        stream = out_proj_residual(merged, wts["w_out"], l, stream, mod, 2, geom)

        h2, h2_packed = norm_modulate(stream, norm_ffn_g[l], mod, 3, geom, (f32, jnp.uint32))
        stream = moe_ffn_residual(h2, h2_packed, stream, mod, 5, geom, router_w[l], router_b[l], l,
                                  wts["exp_w1"], wts["exp_b1"], wts["exp_w2"], wts["exp_b2"])

    out = final_norm(stream, final_g)
    return out.reshape(n_batch, p_len, d)[:, :n_lat]
```

```python
import functools
import math

import jax
import jax.numpy as jnp
from jax import lax
from jax.experimental import pallas as pl
from jax.experimental.pallas import tpu as pltpu

GRID_W = 64
EPS = 1e-6
N_MOD = 6
N_BRANCH = 4

MLA_HEADS = 8
MLA_NOPE = 128
MLA_ROPE = 64
MLA_V = 128
Q_LORA = 512
KV_LORA = 512
ROPE_BASE = 10000.0

NA_HEADS = 8
NA_DIM = 128
WIN_R = 8
WIN_C = 16

LRU_W = 1024
LRU_BLOCKS = 8
LRU_C = 8.0
CONV_W = 4
CONV_PAD_L = 2

DN_HEADS = 8
DN_DK = 128
DN_DV = 128
DN_CHUNK = 64

N_EXPERTS = 32
TOP_K = 4
D_FF = 640
SWIGLU_ALPHA = 1.702
SWIGLU_LIMIT = 7.0

ROW_TILE = 256
LANES = 128
NEG_BIG = -1e30
VMEM_LIMIT = 56 * 1024 * 1024

f32 = jnp.float32
bf16 = jnp.bfloat16


def _params(sem, vmem=VMEM_LIMIT):
    return pltpu.CompilerParams(dimension_semantics=sem, vmem_limit_bytes=vmem)


def _sigmoid(x):
    return 1.0 / (1.0 + jnp.exp(-x))


def _silu(x):
    return x * _sigmoid(x)


def _softplus(x):
    return jnp.maximum(x, 0.0) + jnp.log1p(jnp.exp(-jnp.abs(x)))


def _gelu_tanh(x):
    return 0.5 * x * (1.0 + jnp.tanh(math.sqrt(2.0 / math.pi) * (x + 0.044715 * x * x * x)))


def _mm_kernel(x_ref, w_ref, o_ref):
    o_ref[...] = jnp.dot(x_ref[...], w_ref[...], preferred_element_type=f32).astype(o_ref.dtype)


def matmul(x, w, layer, col0, n, out_dtype, tm, tn, name):
    m, k = x.shape
    while m % tm:
        tm //= 2
    tn = min(tn, n)
    assert tm % 8 == 0 and n % tn == 0 and col0 % tn == 0, (m, n, tm, tn, col0)
    cb = col0 // tn
    return pl.pallas_call(
        _mm_kernel,
        out_shape=jax.ShapeDtypeStruct((m, n), out_dtype),
        grid=(n // tn, m // tm),
        in_specs=[pl.BlockSpec((tm, k), lambda j, i: (i, 0)),
                  pl.BlockSpec((None, k, tn), lambda j, i: (layer, 0, cb + j))],
        out_specs=pl.BlockSpec((tm, tn), lambda j, i: (i, j)),
        compiler_params=_params(("parallel", "parallel")),
        name=name,
    )(x, w)


def _ada_kernel(c_ref, w_ref, b_ref, o_ref):
    a = _silu(c_ref[...])
    o_ref[...] = jnp.dot(a, w_ref[...], preferred_element_type=f32,
                         precision=lax.Precision.HIGHEST) + b_ref[...]


def ada_modulation(cond, ada_w, ada_b):
    depth, d, n = ada_w.shape
    r = cond.shape[0]
    tn = 512
    return pl.pallas_call(
        _ada_kernel,
        out_shape=jax.ShapeDtypeStruct((depth, r, n), f32),
        grid=(depth, n // tn),
        in_specs=[pl.BlockSpec((r, d), lambda l, j: (0, 0)),
                  pl.BlockSpec((None, d, tn), lambda l, j: (l, 0, j)),
                  pl.BlockSpec((None, 1, tn), lambda l, j: (l, 0, j))],
        out_specs=pl.BlockSpec((None, r, tn), lambda l, j: (l, 0, j)),
        compiler_params=_params(("parallel", "parallel")),
        name="ada_modulation",
    )(cond, ada_w, ada_b.reshape(depth, 1, n))


def _mod_row(i, tiles_per_batch, lat_tiles, n_batch):
    return jnp.where(i % tiles_per_batch < lat_tiles, i // tiles_per_batch, n_batch)


def _pack_halves(x):
    m = x.shape[1] // 2
    bits = lambda v: lax.bitcast_convert_type(v.astype(bf16).astype(f32), jnp.uint32)
    return (bits(x[:, :m]) >> 16) | (bits(x[:, m:]) & jnp.uint32(0xFFFF0000))


def _unpack_halves(p):
    return (lax.bitcast_convert_type(p << 16, f32),
            lax.bitcast_convert_type(p & jnp.uint32(0xFFFF0000), f32))


def _norm_mod_kernel(x_ref, g_ref, mod_ref, *o_refs, shift_row):
    x = x_ref[...]
    y = x * lax.rsqrt(jnp.mean(x * x, axis=-1, keepdims=True) + EPS) * g_ref[...]
    h = y * (1.0 + mod_ref[shift_row + 1:shift_row + 2, :]) + mod_ref[shift_row:shift_row + 1, :]
    for o_ref in o_refs:
        o_ref[...] = _pack_halves(h) if o_ref.dtype == jnp.uint32 else h.astype(o_ref.dtype)


def norm_modulate(x, g, mod, shift_row, geom, out_dtypes):
    t, d = x.shape
    n_batch, tpb, lat = geom
    mrow = functools.partial(_mod_row, tiles_per_batch=tpb, lat_tiles=lat, n_batch=n_batch)
    widths = [d // 2 if dt == jnp.uint32 else d for dt in out_dtypes]
    outs = pl.pallas_call(
        functools.partial(_norm_mod_kernel, shift_row=shift_row),
        out_shape=[jax.ShapeDtypeStruct((t, w), dt) for w, dt in zip(widths, out_dtypes)],
        grid=(t // ROW_TILE,),
        in_specs=[pl.BlockSpec((ROW_TILE, d), lambda i: (i, 0)),
                  pl.BlockSpec((1, d), lambda i: (0, 0)),
                  pl.BlockSpec((None, N_MOD, d), lambda i: (mrow(i), 0, 0))],
        out_specs=[pl.BlockSpec((ROW_TILE, w), lambda i: (i, 0)) for w in widths],
        compiler_params=_params(("parallel",)),
        name="norm_modulate",
    )(x, g.reshape(1, d), mod)
    return outs


def _final_norm_kernel(x_ref, g_ref, o_ref):
    x = x_ref[...]
    o_ref[...] = x * lax.rsqrt(jnp.mean(x * x, axis=-1, keepdims=True) + EPS) * g_ref[...]


def final_norm(x, g):
    t, d = x.shape
    return pl.pallas_call(
        _final_norm_kernel,
        out_shape=jax.ShapeDtypeStruct((t, d), f32),
        grid=(t // ROW_TILE,),
        in_specs=[pl.BlockSpec((ROW_TILE, d), lambda i: (i, 0)),
                  pl.BlockSpec((1, d), lambda i: (0, 0))],
        out_specs=pl.BlockSpec((ROW_TILE, d), lambda i: (i, 0)),
        compiler_params=_params(("parallel",)),
        name="final_norm",
    )(x, g.reshape(1, d))


def _mla_prep_kernel(p_ref, qg_ref, kvg_ref, wq_ref, wkv_ref, cos_ref, sin_ref,
                     q_ref, k_ref, v_ref):
    def rms(v, g):
        return v * lax.rsqrt(jnp.mean(v * v, axis=-1, keepdims=True) + EPS) * g

    cos, sin = cos_ref[...], sin_ref[...]

    def rope(t):
        return t * cos + pltpu.roll(t, MLA_ROPE, axis=1) * sin

    qn = rms(p_ref[:, :Q_LORA], qg_ref[...]).astype(bf16)
    kvn = rms(p_ref[:, Q_LORA:Q_LORA + KV_LORA], kvg_ref[...]).astype(bf16)
    scale = (MLA_NOPE + MLA_ROPE) ** -0.5 * math.log2(math.e)
    q = jnp.dot(qn, wq_ref[...], preferred_element_type=f32) * scale
    hw = 2 * LANES
    for h in range(MLA_HEADS):
        q_ref[:, h * hw:h * hw + LANES] = q[:, h * hw:h * hw + LANES].astype(bf16)
        q_ref[:, h * hw + LANES:(h + 1) * hw] = rope(q[:, h * hw + LANES:(h + 1) * hw]).astype(bf16)
    kv = jnp.dot(kvn, wkv_ref[...], preferred_element_type=f32)
    kpe = rope(p_ref[:, Q_LORA + KV_LORA:Q_LORA + KV_LORA + LANES]).astype(bf16)
    nk = MLA_HEADS * MLA_NOPE
    ones_col = jnp.where(lax.broadcasted_iota(jnp.int32, (kpe.shape[0], LANES), 1) == 0, 1.0, 0.0).astype(bf16)
    for h in range(MLA_HEADS):
        k_ref[:, h * hw:h * hw + LANES] = kv[:, h * MLA_NOPE:(h + 1) * MLA_NOPE].astype(bf16)
        k_ref[:, h * hw + LANES:(h + 1) * hw] = kpe
        v_ref[:, h * hw:h * hw + LANES] = kv[:, nk + h * MLA_V:nk + (h + 1) * MLA_V].astype(bf16)
        v_ref[:, h * hw + LANES:(h + 1) * hw] = ones_col


def mla_prep(proj, qn_g, kvn_g, wq, wkv, layer, cos_t, sin_t, tiles_per_batch):
    t = proj.shape[0]
    pw = proj.shape[1]
    qw = MLA_HEADS * 2 * LANES
    return pl.pallas_call(
        _mla_prep_kernel,
        out_shape=[jax.ShapeDtypeStruct((t, qw), bf16),
                   jax.ShapeDtypeStruct((t, qw), bf16),
                   jax.ShapeDtypeStruct((t, qw), bf16)],
        grid=(t // ROW_TILE,),
        in_specs=[pl.BlockSpec((ROW_TILE, pw), lambda i: (i, 0)),
                  pl.BlockSpec((1, Q_LORA), lambda i: (0, 0)),
                  pl.BlockSpec((1, KV_LORA), lambda i: (0, 0)),
                  pl.BlockSpec((None,) + wq.shape[1:], lambda i: (layer, 0, 0)),
                  pl.BlockSpec((None,) + wkv.shape[1:], lambda i: (layer, 0, 0)),
                  pl.BlockSpec((ROW_TILE, LANES), lambda i: (i % tiles_per_batch, 0)),
                  pl.BlockSpec((ROW_TILE, LANES), lambda i: (i % tiles_per_batch, 0))],
        out_specs=[pl.BlockSpec((ROW_TILE, qw), lambda i: (i, 0)),
                   pl.BlockSpec((ROW_TILE, qw), lambda i: (i, 0)),
                   pl.BlockSpec((ROW_TILE, qw), lambda i: (i, 0))],
        compiler_params=_params(("parallel",)),
        name="mla_prep",
    )(proj, qn_g.reshape(1, -1), kvn_g.reshape(1, -1), wq, wkv, cos_t, sin_t)


MLA_Q_TILE = 2048
MLA_SUB_TILE = 256


def _mla_attn_kernel(q_ref, k_ref, v_ref, o_ref):
    for r0 in range(0, q_ref.shape[0], MLA_SUB_TILE):
        rows = slice(r0, r0 + MLA_SUB_TILE)
        s = lax.dot_general(q_ref[rows, :], k_ref[...], (((1,), (1,)), ((), ())), preferred_element_type=f32)
        p = jnp.exp2((s - jnp.max(s, axis=-1, keepdims=True)).astype(bf16))
        ov = jnp.dot(p, v_ref[...], preferred_element_type=f32)
        o_ref[rows, :] = (ov[:, :MLA_V] / ov[:, MLA_V:MLA_V + 1]).astype(o_ref.dtype)


def mla_attention(q, k, v, n_batch, n_lat):
    p_len = q.shape[1]
    n_ctx = p_len - n_lat
    tq = MLA_Q_TILE
    while n_lat % tq:
        tq //= 2
    assert tq % MLA_SUB_TILE == 0
    assert n_lat % n_ctx == 0
    cb = n_lat // n_ctx
    o_lat = pl.pallas_call(
        _mla_attn_kernel,
        out_shape=jax.ShapeDtypeStruct((n_batch, n_lat, MLA_HEADS * MLA_V), bf16),
        grid=(n_batch, MLA_HEADS, n_lat // tq),
        in_specs=[pl.BlockSpec((None, tq, 2 * LANES), lambda b, h, i: (b, i, h)),
                  pl.BlockSpec((None, p_len, 2 * LANES), lambda b, h, i: (b, 0, h)),
                  pl.BlockSpec((None, p_len, 2 * LANES), lambda b, h, i: (b, 0, h))],
        out_specs=pl.BlockSpec((None, tq, MLA_V), lambda b, h, i: (b, i, h)),
        compiler_params=_params(("parallel", "parallel", "arbitrary")),
        name="mla_attention",
    )(q, k, v)
    o_ctx = pl.pallas_call(
        _mla_attn_kernel,
        out_shape=jax.ShapeDtypeStruct((n_batch, n_ctx, MLA_HEADS * MLA_V), bf16),
        grid=(n_batch, MLA_HEADS),
        in_specs=[pl.BlockSpec((None, n_ctx, 2 * LANES), lambda b, h: (b, cb, h)),
                  pl.BlockSpec((None, n_ctx, 2 * LANES), lambda b, h: (b, cb, h)),
                  pl.BlockSpec((None, n_ctx, 2 * LANES), lambda b, h: (b, cb, h))],
        out_specs=pl.BlockSpec((None, n_ctx, MLA_V), lambda b, h: (b, 0, h)),
        compiler_params=_params(("parallel", "parallel")),
        name="mla_attention_ctx",
    )(q, k, v)
    return jnp.concatenate([o_lat, o_ctx], axis=1)


NA_QROWS = 8
NA_BAND = 16


def _na_geometry(rows):
    assert rows % NA_QROWS == 0 and rows >= NA_BAND, rows
    nblk = rows // NA_QROWS
    starts = [min(max(NA_QROWS * i - WIN_R // 2, 0), rows - NA_BAND) for i in range(nblk)]
    if nblk <= 3:
        reps, pat = list(range(nblk)), list(range(nblk))
    else:
        reps = [0, 1, nblk - 1]
        pat = [0] + [1] * (nblk - 2) + [2]
        for i in range(1, nblk - 1):
            assert starts[i] == NA_QROWS * i - WIN_R // 2
    return nblk, starts, reps, pat


def na_bias_tables(rpb, rows):
    _, starts, reps, _ = _na_geometry(rows)
    nh = rpb.shape[0]
    wr = min(WIN_R, rows)
    n_off = 2 * WIN_R - 1
    qc = jnp.arange(GRID_W)
    cs = jnp.clip(qc - WIN_C // 2, 0, GRID_W - WIN_C)
    kc = jnp.arange(GRID_W)
    col_ok = (kc[None, :] >= cs[:, None]) & (kc[None, :] < cs[:, None] + WIN_C)
    col_off = jnp.clip(kc[None, :] - qc[:, None] + (WIN_C - 1), 0, 2 * WIN_C - 2)
    toe = jnp.where(col_ok[None, None], rpb.astype(f32)[:, :, col_off], NEG_BIG)
    toe = jnp.concatenate([toe, jnp.full((nh, 1, GRID_W, GRID_W), NEG_BIG, f32)], axis=1)
    sel = []
    for i in reps:
        qr = NA_QROWS * i + jnp.arange(NA_QROWS)
        kr = starts[i] + jnp.arange(NA_BAND)
        rs = jnp.clip(qr - wr // 2, 0, rows - wr)
        row_ok = (kr[None, :] >= rs[:, None]) & (kr[None, :] < rs[:, None] + wr)
        row_off = jnp.where(row_ok, kr[None, :] - qr[:, None] + (WIN_R - 1), n_off)
        sel.append(jax.nn.one_hot(row_off, n_off + 1, dtype=f32))
    sel = jnp.stack(sel, axis=0)
    tab = jnp.einsum("pakd,hdqc->phaqkc", sel, toe, precision=lax.Precision.HIGHEST)
    return tab.reshape(len(reps), nh, NA_QROWS * GRID_W, NA_BAND * GRID_W)


def _na_kernel(q_ref, k_ref, v_ref, bias_ref, o_ref, *, n_lat, starts, pat):
    scale = NA_DIM ** -0.5
    k_ctx = k_ref[n_lat:, :]
    v_ctx = v_ref[n_lat:, :]
    nt = (((1,), (1,)), ((), ()))
    qt = NA_QROWS * GRID_W
    bt = NA_BAND * GRID_W
    for i, (st, pt) in enumerate(zip(starts, pat)):
        q = q_ref[i * qt:(i + 1) * qt, :]
        kb = k_ref[st * GRID_W:st * GRID_W + bt, :]
        vb = v_ref[st * GRID_W:st * GRID_W + bt, :]
        s_loc = lax.dot_general(q, kb, nt, preferred_element_type=f32) * scale + bias_ref[pt]
        s_ctx = lax.dot_general(q, k_ctx, nt, preferred_element_type=f32) * scale
        m = jnp.maximum(jnp.max(s_loc, axis=-1, keepdims=True), jnp.max(s_ctx, axis=-1, keepdims=True))
        p_loc = jnp.exp(s_loc - m)
        p_ctx = jnp.exp(s_ctx - m)
        l = jnp.sum(p_loc, axis=-1, keepdims=True) + jnp.sum(p_ctx, axis=-1, keepdims=True)
        o = (jnp.dot(p_loc.astype(bf16), vb, preferred_element_type=f32)
             + jnp.dot(p_ctx.astype(bf16), v_ctx, preferred_element_type=f32))
        o_ref[i * qt:(i + 1) * qt, :] = (o / l).astype(o_ref.dtype)
    qz = q_ref[n_lat:, :]
    s = lax.dot_general(qz, k_ctx, nt, preferred_element_type=f32) * scale
    p = jnp.exp(s - jnp.max(s, axis=-1, keepdims=True))
    o = jnp.dot(p.astype(bf16), v_ctx, preferred_element_type=f32) / jnp.sum(p, axis=-1, keepdims=True)
    o_ref[n_lat:, :] = o.astype(o_ref.dtype)


def neighbourhood_attention(qkv, bias, n_batch, n_lat):
    p_len = qkv.shape[1]
    rows = n_lat // GRID_W
    _, starts, _, pat = _na_geometry(rows)
    npat = bias.shape[0]
    kern = functools.partial(_na_kernel, n_lat=n_lat, starts=tuple(starts), pat=tuple(pat))
    hspec = lambda off: pl.BlockSpec((None, p_len, NA_DIM), lambda h, b: (b, 0, off + h))
    return pl.pallas_call(
        kern,
        out_shape=jax.ShapeDtypeStruct((n_batch, p_len, NA_HEADS * NA_DIM), bf16),
        grid=(NA_HEADS, n_batch),
        in_specs=[hspec(0), hspec(NA_HEADS), hspec(2 * NA_HEADS),
                  pl.BlockSpec((npat, None) + bias.shape[2:], lambda h, b: (0, h, 0, 0))],
        out_specs=pl.BlockSpec((None, p_len, NA_DIM), lambda h, b: (b, 0, h)),
        compiler_params=_params(("parallel", "parallel")),
        name="neighbourhood_attention",
    )(qkv, qkv, qkv, bias)


def _segment_conv(x, w_ref, n_lat):
    p_len = x.shape[0]
    row = lax.broadcasted_iota(jnp.int32, (p_len, 1), 0)
    local = jnp.where(row < n_lat, row, row - n_lat)
    seg_len = jnp.where(row < n_lat, n_lat, p_len - n_lat)
    y = x * w_ref[CONV_PAD_L:CONV_PAD_L + 1, :]
    for j in range(CONV_W):
        off = j - CONV_PAD_L
        if off == 0:
            continue
        shifted = pltpu.roll(x, (-off) % p_len, axis=0)
        ok = (local + off >= 0) & (local + off < seg_len)
        y = y + jnp.where(ok, shifted, 0.0) * w_ref[j:j + 1, :]
    return y


SCAN_CHUNK = 256


SUBLANES = 8


def _chunk_scan(a, b, reverse, h_prev):
    n = a.shape[0]
    sub = lax.broadcasted_iota(jnp.int32, (n, 1), 0) % SUBLANES
    k = 1
    while k < SUBLANES:
        if reverse:
            a_s = pltpu.roll(a, n - k, axis=0)
            b_s = pltpu.roll(b, n - k, axis=0)
            ok = sub < SUBLANES - k
        else:
            a_s = pltpu.roll(a, k, axis=0)
            b_s = pltpu.roll(b, k, axis=0)
            ok = sub >= k
        b = b + a * jnp.where(ok, b_s, 0.0)
        a = a * jnp.where(ok, a_s, 1.0)
        k *= 2
    tiles = list(range(n // SUBLANES))
    out = [None] * len(tiles)
    for j in (tiles[::-1] if reverse else tiles):
        rows = slice(j * SUBLANES, (j + 1) * SUBLANES)
        h = b[rows] + a[rows] * h_prev
        h_prev = h[0:1] if reverse else h[SUBLANES - 1:SUBLANES]
        out[j] = h
    return jnp.concatenate(out, axis=0), h_prev


def _lru_kernel(lu_ref, ly_ref, cw_ref, cb_ref, wa_ref, ba_ref, wx_ref, bx_ref, lam_ref, o_ref,
                u_scr, a_scr, b_scr, h_scr, *, n_lat):
    p_len = lu_ref.shape[0]
    u_scr[...] = _segment_conv(lu_ref[...], cw_ref, n_lat) + cb_ref[...]
    segments = ((n_lat, p_len - n_lat), (0, n_lat))

    for d, reverse in enumerate((False, True)):
        ub = u_scr[...].astype(bf16)
        r = _sigmoid(jnp.dot(ub, wa_ref[d].astype(bf16), preferred_element_type=f32) + ba_ref[d:d + 1, :])
        g = _sigmoid(jnp.dot(ub, wx_ref[d].astype(bf16), preferred_element_type=f32) + bx_ref[d:d + 1, :])
        log_a = (-LRU_C * r) * _softplus(-lam_ref[d:d + 1, :])
        a_scr[...] = jnp.exp(log_a)
        th = jnp.tanh(log_a)
        b_scr[...] = jnp.sqrt(-2.0 * th / (1.0 - th)) * (g * u_scr[...])

        state = jnp.zeros((1, LANES), f32)
        for start, length in segments:
            n_chunks = length // SCAN_CHUNK

            def body(c, h_prev, start=start, n_chunks=n_chunks, reverse=reverse):
                ci = (n_chunks - 1 - c) if reverse else c
                rows = pl.ds(pl.multiple_of(start + ci * SCAN_CHUNK, SCAN_CHUNK), SCAN_CHUNK)
                h, h_last = _chunk_scan(a_scr[rows, :], b_scr[rows, :], reverse, h_prev)
                if d == 0:
                    h_scr[rows, :] = h
                else:
                    h_scr[rows, :] = h_scr[rows, :] + h
                return h_last

            state = lax.fori_loop(0, n_chunks, body, state)

    o_ref[...] = (h_scr[...] * _gelu_tanh(ly_ref[...])).astype(o_ref.dtype)


def rglru(luy, conv_w, conv_b, wa, ba, wx, bx, lam, n_batch, n_lat):
    p_len = luy.shape[1]
    nblk = LRU_W // LANES
    assert LRU_W // LRU_BLOCKS == LANES and n_lat % SCAN_CHUNK == 0 and (p_len - n_lat) % SCAN_CHUNK == 0
    vec = lambda rows: pl.BlockSpec((rows, LANES), lambda b, j: (0, j))
    return pl.pallas_call(
        functools.partial(_lru_kernel, n_lat=n_lat),
        out_shape=jax.ShapeDtypeStruct((n_batch, p_len, LRU_W), bf16),
        grid=(n_batch, nblk),
        in_specs=[pl.BlockSpec((None, p_len, LANES), lambda b, j: (b, 0, j)),
                  pl.BlockSpec((None, p_len, LANES), lambda b, j: (b, 0, nblk + j)),
                  vec(CONV_W), vec(1),
                  pl.BlockSpec((2, None, LANES, LANES), lambda b, j: (0, j, 0, 0)), vec(2),
                  pl.BlockSpec((2, None, LANES, LANES), lambda b, j: (0, j, 0, 0)), vec(2),
                  vec(2)],
        out_specs=pl.BlockSpec((None, p_len, LANES), lambda b, j: (b, 0, j)),
        scratch_shapes=[pltpu.VMEM((p_len, LANES), f32) for _ in range(4)],
        compiler_params=_params(("parallel", "parallel")),
        name="rglru",
    )(luy, luy, conv_w, conv_b.reshape(1, LRU_W), wa, ba, wx, bx, lam)


def _dn_pre_kernel(x_ref, w_ref, o_ref, *, n_lat):
    j = pl.program_id(1)
    u = _silu(_segment_conv(x_ref[...], w_ref, n_lat))
    nrm = u * lax.rsqrt(jnp.sum(u * u, axis=-1, keepdims=True) + EPS)
    qk_scale = jnp.where(j < DN_HEADS, DN_DK ** -0.5, 1.0)
    o_ref[...] = jnp.where(j < 2 * DN_HEADS, nrm * qk_scale, u).astype(o_ref.dtype)


def deltanet_pre(dqkv, conv_w, n_batch, n_lat, col_blocks):
    p_len = dqkv.shape[1]
    return pl.pallas_call(
        functools.partial(_dn_pre_kernel, n_lat=n_lat),
        out_shape=jax.ShapeDtypeStruct((n_batch, p_len, col_blocks * LANES), bf16),
        grid=(n_batch, col_blocks),
        in_specs=[pl.BlockSpec((None, p_len, LANES), lambda b, j: (b, 0, j)),
                  pl.BlockSpec((CONV_W, LANES), lambda b, j: (0, j))],
        out_specs=pl.BlockSpec((None, p_len, LANES), lambda b, j: (b, 0, j)),
        compiler_params=_params(("parallel", "parallel")),
        name="deltanet_pre",
    )(dqkv, conv_w)


def _dn_gate_kernel(ab_ref, alog_ref, dt_ref, o_ref):
    nh2 = 2 * DN_HEADS
    a_raw = ab_ref[:, :nh2]
    b_raw = ab_ref[:, nh2:2 * nh2]
    g = -jnp.exp(alog_ref[...]) * _softplus(a_raw + dt_ref[...])
    beta = _sigmoid(b_raw)
    n = g.shape[0]
    r = lax.broadcasted_iota(jnp.int32, (n, n), 0)
    c = lax.broadcasted_iota(jnp.int32, (n, n), 1)
    same = (r // DN_CHUNK) == (c // DN_CHUNK)
    hi = lax.Precision.HIGHEST
    pre = jnp.dot(jnp.where(same & (c <= r), 1.0, 0.0), g, preferred_element_type=f32, precision=hi)
    suf = jnp.dot(jnp.where(same & (c >= r), 1.0, 0.0), g, preferred_element_type=f32, precision=hi)
    tot = jnp.dot(jnp.where(same, 1.0, 0.0), g, preferred_element_type=f32, precision=hi)
    pad = jnp.zeros((n, LANES - 3 * nh2), f32)
    o_ref[...] = jnp.concatenate([pre[:, :DN_HEADS], suf[:, DN_HEADS:], beta, tot, pad], axis=-1)


def deltanet_gates(ab, col_block, a_log, dt_bias):
    t = ab.shape[0]
    nh2 = 2 * DN_HEADS
    return pl.pallas_call(
        _dn_gate_kernel,
        out_shape=jax.ShapeDtypeStruct((t, LANES), f32),
        grid=(t // ROW_TILE,),
        in_specs=[pl.BlockSpec((ROW_TILE, LANES), lambda i: (i, col_block)),
                  pl.BlockSpec((1, nh2), lambda i: (0, 0)),
                  pl.BlockSpec((1, nh2), lambda i: (0, 0))],
        out_specs=pl.BlockSpec((ROW_TILE, LANES), lambda i: (i, 0)),
        compiler_params=_params(("parallel",)),
        name="deltanet_gates",
    )(ab, a_log.reshape(1, nh2), dt_bias.reshape(1, nh2))


def _tri_inverse_minus_eye(low):
    c = low.shape[-1]
    bmm = functools.partial(jnp.einsum, "gij,gjk->gik", preferred_element_type=f32)
    x = -low
    acc = x
    k = 2
    while k < c:
        xb = x.astype(bf16)
        x = bmm(xb, xb)
        acc = acc + x + bmm(x.astype(bf16), acc.astype(bf16))
        k *= 2
    return acc


DN_GROUP_MAX = 17


def _dn_kernel(q_ref, k_ref, v_ref, z_ref, gcol_ref, grow_ref, ng_ref, o_ref,
               lhs_scr, inc_scr, gend_scr, acc_scr, *, n_lat, grp):
    p_len = q_ref.shape[0]
    cs = DN_CHUNK
    n_chunks = p_len // cs
    nc_lat = n_lat // cs
    nc_ctx = n_chunks - nc_lat
    nb = 2 * grp
    ri = lax.broadcasted_iota(jnp.int32, (nb, cs, cs), 1)
    ci = lax.broadcasted_iota(jnp.int32, (nb, cs, cs), 2)
    rev = lax.broadcasted_iota(jnp.int32, (nb, cs, cs), 0) >= grp
    ahead = jnp.where(rev, ri - ci, ci - ri)
    incl = ahead <= 0
    strict = ahead < 0

    def phase1(g, carry):
        c0 = g * grp
        rows = pl.ds(pl.multiple_of(c0 * cs, cs * grp), cs * grp)

        def both(x):
            x = x.astype(f32).reshape(grp, cs, x.shape[-1])
            return jnp.concatenate([x, x], axis=0)

        def per_dir(col):
            return jnp.concatenate([gcol_ref[rows, col + d:col + d + 1].reshape(grp, cs, 1) for d in range(2)], axis=0)

        q, k, v = both(q_ref[rows, :]), both(k_ref[rows, :]), both(v_ref[rows, :])
        gc, beta, gtot = per_dir(0), per_dir(2), per_dir(4)
        gr = jnp.concatenate([grow_ref[d, pl.ds(c0, grp), :] for d in range(2)], axis=0)[:, None, :]
        decay = jnp.where(incl, jnp.exp(jnp.where(incl, gc - gr, 0.0)), 0.0)
        e_gc = jnp.exp(gc)
        kb = k * beta
        gram = jnp.einsum("gik,gjk->gij", jnp.concatenate([kb, q], axis=1).astype(bf16), k.astype(bf16),
                          preferred_element_type=f32)
        t_m1 = _tri_inverse_minus_eye(jnp.where(strict, gram[:, :cs] * decay, 0.0))
        rhs = jnp.concatenate([v * beta, kb * e_gc], axis=2)
        w = rhs + jnp.einsum("gij,gjk->gik", t_m1.astype(bf16), rhs.astype(bf16), preferred_element_type=f32)
        k_dec = (k * jnp.exp(gtot - gc)).astype(bf16)
        a_in = (gram[:, cs:] * decay).astype(bf16)
        w_hi = w.astype(bf16)
        wv_lo = (w[:, :, :DN_DV] - w_hi[:, :, :DN_DV].astype(f32)).astype(bf16)
        kdt_w = jnp.einsum("gck,gcn->gkn", k_dec, w_hi, preferred_element_type=f32)
        s_inc = kdt_w[:, :, :DN_DV] + jnp.einsum("gck,gcn->gkn", k_dec, wv_lo, preferred_element_type=f32)
        ain_w = jnp.einsum("gij,gjn->gin", a_in, w_hi, preferred_element_type=f32)
        lhs = jnp.concatenate([-kdt_w[:, :, DN_DV:], q * e_gc - ain_w[:, :, DN_DV:]], axis=1).astype(bf16)
        g_end = jnp.broadcast_to(jnp.exp(gtot[:, 0:1, :]), (nb, 1, DN_DV))
        o_local = ain_w[:, :, :DN_DV]
        acc_scr[rows, :] = (o_local[:grp] + o_local[grp:]).reshape(grp * cs, DN_DV)
        for d in range(2):
            sl = slice(d * grp, (d + 1) * grp)
            dst = pl.ds(c0, grp)
            lhs_scr[d, dst] = lhs[sl]
            inc_scr[d, dst] = s_inc[sl]
            gend_scr[d, dst] = g_end[sl]
        return carry

    lax.fori_loop(0, n_chunks // grp, phase1, 0)

    def phase2(t, states):
        in_ctx = t < nc_ctx
        chunk = (jnp.where(in_ctx, nc_lat + t, t - nc_ctx),
                 jnp.where(in_ctx, n_chunks - 1 - t, nc_lat - 1 - (t - nc_ctx)))
        new_states = []
        for d in range(2):
            c = chunk[d]
            s = states[d]
            a = jnp.dot(lhs_scr[d, c], s.astype(bf16), preferred_element_type=f32)
            rows = pl.ds(pl.multiple_of(c * cs, cs), cs)
            acc_scr[rows, :] = acc_scr[rows, :] + a[DN_DK:]
            new_states.append(s * gend_scr[d, c] + inc_scr[d, c] + a[:DN_DK])
        return tuple(new_states)

    zero = jnp.zeros((DN_DK, DN_DV), f32)
    lax.fori_loop(0, n_chunks, phase2, (zero, zero))

    o = acc_scr[...]
    y = o * lax.rsqrt(jnp.mean(o * o, axis=-1, keepdims=True) + EPS) * ng_ref[...]
    o_ref[...] = (y * _silu(z_ref[...])).astype(o_ref.dtype)


def deltanet(qkv, dz_src, z_col0, gcol, grow, norm_g, n_batch, n_lat):
    p_len = qkv.shape[1]
    n_chunks = p_len // DN_CHUNK
    assert n_lat % DN_CHUNK == 0 and p_len % DN_CHUNK == 0
    grp = max(g for g in range(1, DN_GROUP_MAX + 1) if n_chunks % g == 0)
    hspec = lambda off: pl.BlockSpec((None, p_len, LANES), lambda b, h: (b, 0, off + h))
    return pl.pallas_call(
        functools.partial(_dn_kernel, n_lat=n_lat, grp=grp),
        out_shape=jax.ShapeDtypeStruct((n_batch, p_len, DN_HEADS * DN_DV), bf16),
        grid=(n_batch, DN_HEADS),
        in_specs=[hspec(0), hspec(DN_HEADS), hspec(2 * DN_HEADS),
                  pl.BlockSpec((None, p_len, LANES), lambda b, h: (b, 0, z_col0 + h)),
                  pl.BlockSpec((None, None, p_len, 8), lambda b, h: (b, h, 0, 0)),
                  pl.BlockSpec((None, None, 2, p_len // DN_CHUNK, DN_CHUNK), lambda b, h: (b, h, 0, 0, 0)),
                  pl.BlockSpec((1, DN_DV), lambda b, h: (0, 0))],
        out_specs=pl.BlockSpec((None, p_len, LANES), lambda b, h: (b, 0, h)),
        scratch_shapes=[pltpu.VMEM((2, n_chunks, DN_DK + DN_CHUNK, DN_DK), bf16),
                        pltpu.VMEM((2, n_chunks, DN_DK, DN_DV), f32),
                        pltpu.VMEM((2, n_chunks, 1, DN_DV), f32),
                        pltpu.VMEM((p_len, DN_DV), f32)],
        compiler_params=_params(("parallel", "parallel")),
        name="deltanet",
    )(qkv, qkv, qkv, dz_src, gcol, grow, norm_g.reshape(1, DN_DV))


def _merge_kernel(*refs):
    o_refs, h_ref, wg_refs, wb_ref, out_ref = (refs[:N_BRANCH], refs[N_BRANCH], refs[N_BRANCH + 1:2 * N_BRANCH + 1],
                                               refs[-2], refs[-1])
    h = h_ref[...]
    acc = None
    for i in range(N_BRANCH):
        y = jnp.dot(o_refs[i][...], wb_ref[i], preferred_element_type=f32)
        y = _sigmoid(jnp.dot(h, wg_refs[i][...], preferred_element_type=f32)) * y
        acc = y if acc is None else acc + y
    out_ref[...] = acc.astype(out_ref.dtype)


def merge_branches(branches, h, w_all, off_gate, w_branch, layer):
    t, bw = branches[0].shape
    d = w_branch.shape[3]
    tm, tn = ROW_TILE * 2, min(512, d)
    while t % tm:
        tm //= 2
    nj = d // tn
    assert off_gate % tn == 0 and d % tn == 0
    bspec = pl.BlockSpec((tm, bw), lambda j, i: (i, 0))
    gspecs = [pl.BlockSpec((None, d, tn),
                           functools.partial(lambda j, i, cb: (layer, 0, cb + j), cb=(off_gate + br * d) // tn))
              for br in range(N_BRANCH)]
    return pl.pallas_call(
        _merge_kernel,
        out_shape=jax.ShapeDtypeStruct((t, d), bf16),
        grid=(nj, t // tm),
        in_specs=([bspec] * N_BRANCH + [pl.BlockSpec((tm, d), lambda j, i: (i, 0))] + gspecs
                  + [pl.BlockSpec((None, N_BRANCH, bw, tn), lambda j, i: (layer, 0, 0, j))]),
        out_specs=pl.BlockSpec((tm, tn), lambda j, i: (i, j)),
        compiler_params=_params(("parallel", "parallel")),
        name="merge_branches",
    )(*branches, h, *([w_all] * N_BRANCH), w_branch)


def _out_proj_kernel(m_ref, w_ref, x_ref, mod_ref, o_ref, *, gate_row):
    y = jnp.dot(m_ref[...], w_ref[...], preferred_element_type=f32)
    o_ref[...] = x_ref[...] + mod_ref[gate_row:gate_row + 1, :] * y


def out_proj_residual(merged, w_out, layer, x, mod, gate_row, geom):
    t, d = x.shape
    n_batch, tpb, lat = geom
    mrow = functools.partial(_mod_row, tiles_per_batch=tpb, lat_tiles=lat, n_batch=n_batch)
    return pl.pallas_call(
        functools.partial(_out_proj_kernel, gate_row=gate_row),
        out_shape=jax.ShapeDtypeStruct((t, d), f32),
        grid=(t // ROW_TILE,),
        in_specs=[pl.BlockSpec((ROW_TILE, d), lambda i: (i, 0)),
                  pl.BlockSpec((None, d, d), lambda i: (layer, 0, 0)),
                  pl.BlockSpec((ROW_TILE, d), lambda i: (i, 0)),
                  pl.BlockSpec((None, N_MOD, d), lambda i: (mrow(i), 0, 0))],
        out_specs=pl.BlockSpec((ROW_TILE, d), lambda i: (i, 0)),
        compiler_params=_params(("parallel",)),
        name="out_proj_residual",
    )(merged, w_out, x, mod)


def _route_kernel(h_ref, rw_ref, rb_ref, sel_ref, cnt_ref, carry):
    i = pl.program_id(0)

    @pl.when(i == 0)
    def _():
        carry[...] = jnp.zeros_like(carry)

    logits = jnp.dot(h_ref[...], rw_ref[...], preferred_element_type=f32,
                     precision=lax.Precision.HIGHEST) + rb_ref[...]
    n = logits.shape[0]
    lane = lax.broadcasted_iota(jnp.int32, (n, LANES), 1).astype(f32)
    work = logits
    vals, idxs, hots = [], [], []
    for _ in range(TOP_K):
        m = jnp.max(work, axis=-1, keepdims=True)
        idx = jnp.min(jnp.where(work == m, lane, float(LANES)), axis=-1, keepdims=True)
        hot = lane == idx
        vals.append(m)
        idxs.append(idx)
        hots.append(hot)
        work = jnp.where(hot, NEG_BIG * 2.0, work)
    exps = [jnp.exp(v - vals[0]) for v in vals]
    den = exps[0]
    for e in exps[1:]:
        den = den + e
    onehot = jnp.zeros((n, LANES), f32)
    for hot in hots:
        onehot = onehot + jnp.where(hot, 1.0, 0.0)
    r = lax.broadcasted_iota(jnp.int32, (n, n), 0)
    c = lax.broadcasted_iota(jnp.int32, (n, n), 1)
    before = jnp.dot(jnp.where(c < r, 1.0, 0.0).astype(bf16), onehot.astype(bf16),
                     preferred_element_type=f32) + carry[...]
    out = jnp.zeros((n, LANES), f32)
    for kk in range(TOP_K):
        rank = jnp.sum(jnp.where(hots[kk], before, 0.0), axis=-1, keepdims=True)
        out = jnp.where(lane == float(kk), idxs[kk], out)
        out = jnp.where(lane == float(TOP_K + kk), rank, out)
        out = jnp.where(lane == float(2 * TOP_K + kk), exps[kk] / den, out)
    sel_ref[...] = out
    carry[...] = carry[...] + jnp.sum(onehot, axis=0, keepdims=True)
    cnt_ref[...] = carry[...]


def moe_route(h, router_w, router_b):
    t, d = h.shape
    e = router_w.shape[1]
    rw = jnp.zeros((d, LANES), f32).at[:, :e].set(router_w)
    rb = jnp.full((1, LANES), NEG_BIG, f32).at[0, :e].set(router_b)
    return pl.pallas_call(
        _route_kernel,
        out_shape=[jax.ShapeDtypeStruct((t, LANES), f32), jax.ShapeDtypeStruct((1, LANES), f32)],
        grid=(t // ROW_TILE,),
        in_specs=[pl.BlockSpec((ROW_TILE, d), lambda i: (i, 0)),
                  pl.BlockSpec((d, LANES), lambda i: (0, 0)),
                  pl.BlockSpec((1, LANES), lambda i: (0, 0))],
        out_specs=[pl.BlockSpec((ROW_TILE, LANES), lambda i: (i, 0)),
                   pl.BlockSpec((1, LANES), lambda i: (0, 0))],
        scratch_shapes=[pltpu.VMEM((1, LANES), f32)],
        compiler_params=_params(("arbitrary",)),
        name="moe_route",
    )(h, rw, rb)


def _dispatch_kernel(pos_hbm, x_ref, init_ref, xs_hbm, pos_smem, sem_pos, sem_rows):
    del init_ref
    i = pl.program_id(0)
    n_sel = ROW_TILE * TOP_K
    cp = pltpu.make_async_copy(pos_hbm.at[i], pos_smem, sem_pos)
    cp.start()
    cp.wait()

    def issue(r, c):
        src = x_ref.at[pl.ds(r, 1)]
        for kk in range(TOP_K):
            pltpu.make_async_copy(src, xs_hbm.at[pl.ds(pos_smem[r * TOP_K + kk], 1)], sem_rows).start(priority=kk % 2)
        return c

    lax.fori_loop(0, ROW_TILE, issue, 0, unroll=4)
    pltpu.make_async_copy(xs_hbm.at[pl.ds(0, n_sel)], xs_hbm.at[pl.ds(0, n_sel)], sem_rows).wait()


def moe_dispatch(h, pos, n_slots):
    t, d = h.shape
    tiles = t // ROW_TILE
    init = jnp.zeros((n_slots, d), h.dtype)
    return pl.pallas_call(
        _dispatch_kernel,
        out_shape=jax.ShapeDtypeStruct((n_slots, d), h.dtype),
        grid=(tiles,),
        in_specs=[pl.BlockSpec(memory_space=pl.ANY),
                  pl.BlockSpec((ROW_TILE, d), lambda i: (i, 0)),
                  pl.BlockSpec(memory_space=pl.ANY)],
        out_specs=pl.BlockSpec(memory_space=pl.ANY),
        scratch_shapes=[pltpu.SMEM((ROW_TILE * TOP_K,), jnp.int32),
                        pltpu.SemaphoreType.DMA, pltpu.SemaphoreType.DMA],
        input_output_aliases={2: 0},
        compiler_params=_params(("arbitrary",)),
        name="moe_dispatch",
    )(pos.reshape(tiles, ROW_TILE * TOP_K), h, init)


def _expert_kernel(te_ref, xs_ref, w1_ref, b1_ref, sel_ref, w2_ref, b2_ref, ys_ref):
    del te_ref
    x = jnp.concatenate(_unpack_halves(xs_ref[...]), axis=1).astype(bf16)
    u = jnp.dot(x, w1_ref[...], preferred_element_type=f32) + b1_ref[...]
    n = u.shape[1]
    lin = pltpu.roll(u, n - 1, axis=1)
    glu = jnp.minimum(u, SWIGLU_LIMIT)
    lin = jnp.clip(lin, -SWIGLU_LIMIT, SWIGLU_LIMIT)
    act = glu * _sigmoid(SWIGLU_ALPHA * glu) * (lin + 1.0)
    even = lax.broadcasted_iota(jnp.int32, u.shape, 1) % 2 == 0
    act = jnp.where(even, act, 0.0).astype(bf16)
    act = jnp.dot(act, sel_ref[...], preferred_element_type=f32).astype(bf16)
    ys_ref[...] = _pack_halves(jnp.dot(act, w2_ref[...], preferred_element_type=f32) + b2_ref[...])


EXPERT_TILE = 512


def moe_experts(xs, tile_expert, layer, w1, b1, w2, b2):
    s, dh = xs.shape
    d = 2 * dh
    ff2 = w1.shape[3]
    ff = ff2 // 2
    sel = (jnp.arange(ff2)[:, None] == 2 * jnp.arange(ff)[None, :]).astype(bf16)
    grid_spec = pltpu.PrefetchScalarGridSpec(
        num_scalar_prefetch=1,
        grid=(s // EXPERT_TILE,),
        in_specs=[pl.BlockSpec((EXPERT_TILE, dh), lambda i, te: (i, 0)),
                  pl.BlockSpec((None, None, d, ff2), lambda i, te: (layer, te[i], 0, 0)),
                  pl.BlockSpec((None, None, 1, ff2), lambda i, te: (layer, te[i], 0, 0)),
                  pl.BlockSpec((ff2, ff), lambda i, te: (0, 0)),
                  pl.BlockSpec((None, None, ff, d), lambda i, te: (layer, te[i], 0, 0)),
                  pl.BlockSpec((None, None, 1, d), lambda i, te: (layer, te[i], 0, 0))],
        out_specs=pl.BlockSpec((EXPERT_TILE, dh), lambda i, te: (i, 0)),
    )
    return pl.pallas_call(
        _expert_kernel,
        out_shape=jax.ShapeDtypeStruct((s, dh), jnp.uint32),
        grid_spec=grid_spec,
        compiler_params=_params(("arbitrary",)),
        name="moe_experts",
    )(tile_expert, xs, w1, b1, sel, w2, b2)


def _combine_kernel(pos_hbm, ys_hbm, x_ref, wt_ref, mod_ref, o_ref, buf, pos_smem, sem_pos, sem_rows, *, gate_row):
    i = pl.program_id(0)
    n_sel = ROW_TILE * TOP_K
    cp = pltpu.make_async_copy(pos_hbm.at[i], pos_smem, sem_pos)
    cp.start()
    cp.wait()

    def issue(r, c):
        for kk in range(TOP_K):
            pltpu.make_async_copy(ys_hbm.at[pl.ds(pos_smem[r * TOP_K + kk], 1)],
                                  buf.at[kk, pl.ds(r, 1)], sem_rows).start(priority=kk % 2)
        return c

    lax.fori_loop(0, ROW_TILE, issue, 0, unroll=4)
    pltpu.make_async_copy(ys_hbm.at[pl.ds(0, n_sel)], ys_hbm.at[pl.ds(0, n_sel)], sem_rows).wait()
    m = buf.shape[2]
    acc_lo = acc_hi = None
    for kk in range(TOP_K):
        wk = wt_ref[:, 2 * TOP_K + kk:2 * TOP_K + kk + 1]
        lo, hi = _unpack_halves(buf[kk])
        acc_lo = wk * lo if acc_lo is None else acc_lo + wk * lo
        acc_hi = wk * hi if acc_hi is None else acc_hi + wk * hi
    o_ref[:, :m] = x_ref[:, :m] + mod_ref[gate_row:gate_row + 1, :m] * acc_lo
    o_ref[:, m:] = x_ref[:, m:] + mod_ref[gate_row:gate_row + 1, m:] * acc_hi


def moe_combine(ys, pos, sel, x, mod, gate_row, geom):
    t, d = x.shape
    tiles = t // ROW_TILE
    n_batch, tpb, lat = geom
    mrow = functools.partial(_mod_row, tiles_per_batch=tpb, lat_tiles=lat, n_batch=n_batch)
    return pl.pallas_call(
        functools.partial(_combine_kernel, gate_row=gate_row),
        out_shape=jax.ShapeDtypeStruct((t, d), f32),
        grid=(tiles,),
        in_specs=[pl.BlockSpec(memory_space=pl.ANY),
                  pl.BlockSpec(memory_space=pl.ANY),
                  pl.BlockSpec((ROW_TILE, d), lambda i: (i, 0)),
                  pl.BlockSpec((ROW_TILE, LANES), lambda i: (i, 0)),
                  pl.BlockSpec((None, N_MOD, d), lambda i: (mrow(i), 0, 0))],
        out_specs=pl.BlockSpec((ROW_TILE, d), lambda i: (i, 0)),
        scratch_shapes=[pltpu.VMEM((TOP_K, ROW_TILE, d // 2), jnp.uint32),
                        pltpu.SMEM((ROW_TILE * TOP_K,), jnp.int32),
                        pltpu.SemaphoreType.DMA, pltpu.SemaphoreType.DMA],
        compiler_params=_params(("arbitrary",)),
        name="moe_combine",
    )(pos.reshape(tiles, ROW_TILE * TOP_K), ys, x, sel, mod)


def moe_ffn_residual(h, h_packed, x, mod, gate_row, geom, router_w, router_b, layer, w1, b1, w2, b2):
    t, d = h.shape
    n_exp = router_w.shape[1]
    sel, counts = moe_route(h, router_w, router_b)
    eidx = sel[:, :TOP_K].astype(jnp.int32)
    rank = sel[:, TOP_K:2 * TOP_K].astype(jnp.int32)
    counts = counts[0, :n_exp].astype(jnp.int32)
    padded = ((counts + EXPERT_TILE - 1) // EXPERT_TILE) * EXPERT_TILE
    ends = jnp.cumsum(padded)
    starts = ends - padded
    pos = starts[eidx] + rank
    assert (t * TOP_K) % EXPERT_TILE == 0
    n_tiles = (t * TOP_K) // EXPERT_TILE + n_exp
    tile_start = jnp.arange(n_tiles, dtype=jnp.int32) * EXPERT_TILE
    tile_expert = jnp.minimum(jnp.sum((ends[None, :] <= tile_start[:, None]).astype(jnp.int32), axis=1), n_exp - 1)
    xs = moe_dispatch(h_packed, pos, n_tiles * EXPERT_TILE)
    ys = moe_experts(xs, tile_expert, layer, w1, b1, w2, b2)
    return moe_combine(ys, pos, sel, x, mod, gate_row, geom)


def _rope_perm(w):
    lead = w.shape[:-1]
    nf = MLA_ROPE // 4
    wr = w.reshape(*lead, 2, 2, nf)
    return jnp.stack([-wr[..., 1, :], wr[..., 0, :]], axis=-2).reshape(*lead, MLA_ROPE)


NA_COLS = 3 * NA_HEADS * NA_DIM
LRU_COLS = 2 * LRU_W
DN_COLS = DN_HEADS * (2 * DN_DK + DN_DV) + DN_HEADS * DN_DV
MLA_COLS = Q_LORA + KV_LORA + 2 * LANES


def regroup_w_in(w_in, d_model):
    sizes = (Q_LORA, KV_LORA, MLA_ROPE, NA_HEADS * NA_DIM, NA_HEADS * NA_DIM, NA_HEADS * NA_DIM, LRU_W, LRU_W,
             DN_HEADS * (2 * DN_DK + DN_DV), DN_HEADS * DN_DV, 2 * DN_HEADS, 2 * DN_HEADS, N_BRANCH * d_model)
    parts, s = [], 0
    for n in sizes:
        parts.append(w_in[:, :, s:s + n])
        s += n
    assert s == w_in.shape[2]
    (w_qc, w_kvc, w_kpe, w_naq, w_nak, w_nav, w_lu, w_ly, w_dqkv, w_dz, w_da, w_db, w_gt) = parts
    pad = jnp.zeros(w_in.shape[:2] + (LANES - 4 * DN_HEADS,), w_in.dtype)
    w = jnp.concatenate([w_naq, w_nak, w_nav, w_lu, w_ly, w_dqkv, w_dz, w_gt,
                         w_qc, w_kvc, w_kpe, _rope_perm(w_kpe), w_da, w_db, pad], axis=2).astype(bf16)
    off_na = 0
    off_lru = off_na + NA_COLS
    off_dn = off_lru + LRU_COLS
    off_gate = off_dn + DN_COLS
    off_mla = off_gate + N_BRANCH * d_model
    assert w.shape[2] == off_mla + MLA_COLS
    return w, (off_na, off_lru, off_dn, off_gate, off_mla)


def rope_tables(n_lat, p_len):
    t = jnp.arange(n_lat)
    row = (t // GRID_W).astype(f32)
    col = (t % GRID_W).astype(f32)
    nf = MLA_ROPE // 4
    inv_freq = ROPE_BASE ** (-jnp.arange(nf, dtype=f32) / nf)
    ang = jnp.concatenate([row[:, None] * inv_freq] * 2 + [col[:, None] * inv_freq] * 2, axis=1)
    zeros = jnp.zeros((n_lat, LANES - MLA_ROPE), f32)
    cos_t = jnp.concatenate([jnp.cos(ang), zeros], axis=1)
    sin_t = jnp.concatenate([jnp.sin(ang), zeros], axis=1)
    n_ctx = p_len - n_lat
    ctx_cos = jnp.concatenate([jnp.ones((n_ctx, MLA_ROPE), f32), jnp.zeros((n_ctx, LANES - MLA_ROPE), f32)], axis=1)
    return (jnp.concatenate([cos_t, ctx_cos], axis=0),
            jnp.concatenate([sin_t, jnp.zeros((n_ctx, LANES), f32)], axis=0))


def prepare_weights(w_in, mla_wq_up, mla_wkv_up, w_branch, w_out, exp_w1, exp_b1, exp_w2, exp_b2, d_model):
    depth = w_in.shape[0]
    w_all, offs = regroup_w_in(w_in, d_model)
    wq = mla_wq_up.reshape(depth, Q_LORA, MLA_HEADS, MLA_NOPE + MLA_ROPE)
    wq = jnp.concatenate([wq, _rope_perm(wq[..., MLA_NOPE:])], axis=-1).reshape(depth, Q_LORA, -1).astype(bf16)
    wkv = mla_wkv_up.reshape(depth, KV_LORA, MLA_HEADS, MLA_NOPE + MLA_V)
    wkv = jnp.concatenate([wkv[..., :MLA_NOPE].reshape(depth, KV_LORA, -1),
                           wkv[..., MLA_NOPE:].reshape(depth, KV_LORA, -1)], axis=2).astype(bf16)
    return dict(w_all=w_all, offs=offs, wq=wq, wkv=wkv, w_branch=w_branch.astype(bf16), w_out=w_out.astype(bf16),
                exp_w1=exp_w1.astype(bf16), exp_b1=exp_b1[:, :, None, :], exp_w2=exp_w2.astype(bf16),
                exp_b2=exp_b2[:, :, None, :])


def token_mixer_branches(h, l, wts, geom, n_lat, cos_t, sin_t, na_bias, mla_qn_g, mla_kvn_g, lru_conv_w, lru_conv_b,
                         lru_wa, lru_ba, lru_wx, lru_bx, lru_lam, dn_conv_w, dn_a_log, dn_dt_bias, dn_norm_g):
    n_batch, tpb, lat_tiles = geom
    t, d = h.shape
    p_len = t // n_batch
    w_all = wts["w_all"]
    off_na, off_lru, off_dn, off_gate, off_mla = wts["offs"]

    p_mla = matmul(h, w_all, l, off_mla, MLA_COLS, f32, 1024, 256, "proj_mla")
    q, k, v = mla_prep(p_mla, mla_qn_g, mla_kvn_g, wts["wq"], wts["wkv"], l, cos_t, sin_t, tpb)
    o_a = mla_attention(q.reshape(n_batch, p_len, -1), k.reshape(n_batch, p_len, -1),
                        v.reshape(n_batch, p_len, -1), n_batch, n_lat)

    p_na = matmul(h, w_all, l, off_na, NA_COLS, bf16, 1024, 512, "proj_na")
    o_b = neighbourhood_attention(p_na.reshape(n_batch, p_len, -1), na_bias, n_batch, n_lat)

    p_lru = matmul(h, w_all, l, off_lru, LRU_COLS, f32, 1024, 512, "proj_lru")
    o_c = rglru(p_lru.reshape(n_batch, p_len, -1), lru_conv_w, lru_conv_b, lru_wa, lru_ba, lru_wx, lru_bx, lru_lam,
                n_batch, n_lat)

    p_dn = matmul(h, w_all, l, off_dn, DN_COLS, f32, 1024, 512, "proj_dn").reshape(n_batch, p_len, -1)
    qkv_dn = deltanet_pre(p_dn, dn_conv_w, n_batch, n_lat, 3 * DN_HEADS)
    gates = deltanet_gates(p_mla, (MLA_COLS - LANES) // LANES, dn_a_log, dn_dt_bias).reshape(n_batch, p_len, LANES)
    nh = DN_HEADS

    def per_head(cols):
        return jnp.transpose(cols.reshape(n_batch, p_len, 2, nh), (0, 3, 1, 2))

    gc_cols = per_head(gates[..., :2 * nh])
    gcol = jnp.concatenate([gc_cols, per_head(gates[..., 2 * nh:4 * nh]), per_head(gates[..., 4 * nh:6 * nh]),
                            jnp.zeros((n_batch, nh, p_len, 2), f32)], axis=-1)
    grow = jnp.transpose(gc_cols, (0, 1, 3, 2)).reshape(n_batch, nh, 2, p_len // DN_CHUNK, DN_CHUNK)
    o_d = deltanet(qkv_dn, p_dn, 3 * DN_HEADS, gcol, grow, dn_norm_g, n_batch, n_lat)

    return o_a, o_b, o_c, o_d


def kernel(x, c, ctx, c_ctx, ada_w, ada_b, norm_mix_g, norm_ffn_g, w_in, mla_qn_g, mla_wq_up, mla_kvn_g, mla_wkv_up, na_rpb, lru_conv_w, lru_conv_b, lru_wa, lru_ba, lru_wx, lru_bx, lru_lam, dn_conv_w, dn_a_log, dn_dt_bias, dn_norm_g, w_branch, w_out, router_w, router_b, exp_w1, exp_b1, exp_w2, exp_b2, final_g):
    n_batch, n_lat, d = x.shape
    n_ctx = ctx.shape[1]
    depth = ada_w.shape[0]
    p_len = n_lat + n_ctx
    t = n_batch * p_len
    assert n_lat % ROW_TILE == 0 and n_ctx % ROW_TILE == 0 and n_lat % GRID_W == 0
    tpb, lat_tiles = p_len // ROW_TILE, n_lat // ROW_TILE
    geom = (n_batch, tpb, lat_tiles)

    stream = jnp.concatenate([x, ctx], axis=1).reshape(t, d)
    pad_rows = (-(n_batch + 1)) % 8
    cond = jnp.concatenate([c, c_ctx[None, :], jnp.zeros((pad_rows, d), f32)], axis=0)
    mods = ada_modulation(cond, ada_w, ada_b)[:, :n_batch + 1].reshape(depth, n_batch + 1, N_MOD, d)
    cos_t, sin_t = rope_tables(n_lat, p_len)

    wts = prepare_weights(w_in, mla_wq_up, mla_wkv_up, w_branch, w_out, exp_w1, exp_b1, exp_w2, exp_b2, d)
    bias_tabs = [na_bias_tables(na_rpb[l], n_lat // GRID_W) for l in range(depth)]

    for l in range(depth):
        mod = mods[l]
        (h,) = norm_modulate(stream, norm_mix_g[l], mod, 0, geom, (bf16,))
        o_a, o_b, o_c, o_d = token_mixer_branches(
            h, l, wts, geom, n_lat, cos_t, sin_t, bias_tabs[l], mla_qn_g[l], mla_kvn_g[l], lru_conv_w[l],
            lru_conv_b[l], lru_wa[l], lru_ba[l], lru_wx[l], lru_bx[l], lru_lam[l], dn_conv_w[l], dn_a_log[l],
            dn_dt_bias[l], dn_norm_g[l])

        merged = merge_branches((o_a.reshape(t, -1), o_b.reshape(t, -1), o_c.reshape(t, -1), o_d.reshape(t, -1)),
                                h, wts["w_all"], wts["offs"][3], wts["w_branch"], l)
        stream = out_proj_residual(merged, wts["w_out"], l, stream, mod, 2, geom)

        h2, h2_packed = norm_modulate(stream, norm_ffn_g[l], mod, 3, geom, (f32, jnp.uint32))
        stream = moe_ffn_residual(h2, h2_packed, stream, mod, 5, geom, router_w[l], router_b[l], l,
                                  wts["exp_w1"], wts["exp_b1"], wts["exp_w2"], wts["exp_b2"])

    out = final_norm(stream, final_g)
    return out.reshape(n_batch, p_len, d)[:, :n_lat]
```

```python
import functools
import math

import jax
import jax.numpy as jnp
from jax import lax
from jax.experimental import pallas as pl
from jax.experimental.pallas import tpu as pltpu

GRID_W = 64
EPS = 1e-6
N_MOD = 6
N_BRANCH = 4

MLA_HEADS = 8
MLA_NOPE = 128
MLA_ROPE = 64
MLA_V = 128
Q_LORA = 512
KV_LORA = 512
ROPE_BASE = 10000.0

NA_HEADS = 8
NA_DIM = 128
WIN_R = 8
WIN_C = 16

LRU_W = 1024
LRU_BLOCKS = 8
LRU_C = 8.0
CONV_W = 4
CONV_PAD_L = 2

DN_HEADS = 8
DN_DK = 128
DN_DV = 128
DN_CHUNK = 64

N_EXPERTS = 32
TOP_K = 4
D_FF = 640
SWIGLU_ALPHA = 1.702
SWIGLU_LIMIT = 7.0

ROW_TILE = 256
LANES = 128
NEG_BIG = -1e30
VMEM_LIMIT = 56 * 1024 * 1024

f32 = jnp.float32
bf16 = jnp.bfloat16


def _params(sem, vmem=VMEM_LIMIT):
    return pltpu.CompilerParams(dimension_semantics=sem, vmem_limit_bytes=vmem)


def _sigmoid(x):
    return 1.0 / (1.0 + jnp.exp(-x))


def _silu(x):
    return x * _sigmoid(x)


def _softplus(x):
    return jnp.maximum(x, 0.0) + jnp.log1p(jnp.exp(-jnp.abs(x)))


def _gelu_tanh(x):
    return 0.5 * x * (1.0 + jnp.tanh(math.sqrt(2.0 / math.pi) * (x + 0.044715 * x * x * x)))


def _mm_kernel(x_ref, w_ref, o_ref):
    o_ref[...] = jnp.dot(x_ref[...], w_ref[...], preferred_element_type=f32).astype(o_ref.dtype)


def matmul(x, w, layer, col0, n, out_dtype, tm, tn, name):
    m, k = x.shape
    while m % tm:
        tm //= 2
    tn = min(tn, n)
    assert tm % 8 == 0 and n % tn == 0 and col0 % tn == 0, (m, n, tm, tn, col0)
    cb = col0 // tn
    return pl.pallas_call(
        _mm_kernel,
        out_shape=jax.ShapeDtypeStruct((m, n), out_dtype),
        grid=(n // tn, m // tm),
        in_specs=[pl.BlockSpec((tm, k), lambda j, i: (i, 0)),
                  pl.BlockSpec((None, k, tn), lambda j, i: (layer, 0, cb + j))],
        out_specs=pl.BlockSpec((tm, tn), lambda j, i: (i, j)),
        compiler_params=_params(("parallel", "parallel")),
        name=name,
    )(x, w)


def _ada_kernel(c_ref, w_ref, b_ref, o_ref):
    a = _silu(c_ref[...])
    o_ref[...] = jnp.dot(a, w_ref[...], preferred_element_type=f32,
                         precision=lax.Precision.HIGHEST) + b_ref[...]


def ada_modulation(cond, ada_w, ada_b):
    depth, d, n = ada_w.shape
    r = cond.shape[0]
    tn = 512
    return pl.pallas_call(
        _ada_kernel,
        out_shape=jax.ShapeDtypeStruct((depth, r, n), f32),
        grid=(depth, n // tn),
        in_specs=[pl.BlockSpec((r, d), lambda l, j: (0, 0)),
                  pl.BlockSpec((None, d, tn), lambda l, j: (l, 0, j)),
                  pl.BlockSpec((None, 1, tn), lambda l, j: (l, 0, j))],
        out_specs=pl.BlockSpec((None, r, tn), lambda l, j: (l, 0, j)),
        compiler_params=_params(("parallel", "parallel")),
        name="ada_modulation",
    )(cond, ada_w, ada_b.reshape(depth, 1, n))


def _mod_row(i, tiles_per_batch, lat_tiles, n_batch):
    return jnp.where(i % tiles_per_batch < lat_tiles, i // tiles_per_batch, n_batch)


def _pack_halves(x):
    m = x.shape[1] // 2
    bits = lambda v: lax.bitcast_convert_type(v.astype(bf16).astype(f32), jnp.uint32)
    return (bits(x[:, :m]) >> 16) | (bits(x[:, m:]) & jnp.uint32(0xFFFF0000))


def _unpack_halves(p):
    return (lax.bitcast_convert_type(p << 16, f32),
            lax.bitcast_convert_type(p & jnp.uint32(0xFFFF0000), f32))


def _norm_mod_kernel(x_ref, g_ref, mod_ref, *o_refs, shift_row):
    x = x_ref[...]
    y = x * lax.rsqrt(jnp.mean(x * x, axis=-1, keepdims=True) + EPS) * g_ref[...]
    h = y * (1.0 + mod_ref[shift_row + 1:shift_row + 2, :]) + mod_ref[shift_row:shift_row + 1, :]
    for o_ref in o_refs:
        o_ref[...] = _pack_halves(h) if o_ref.dtype == jnp.uint32 else h.astype(o_ref.dtype)


def norm_modulate(x, g, mod, shift_row, geom, out_dtypes):
    t, d = x.shape
    n_batch, tpb, lat = geom
    mrow = functools.partial(_mod_row, tiles_per_batch=tpb, lat_tiles=lat, n_batch=n_batch)
    widths = [d // 2 if dt == jnp.uint32 else d for dt in out_dtypes]
    outs = pl.pallas_call(
        functools.partial(_norm_mod_kernel, shift_row=shift_row),
        out_shape=[jax.ShapeDtypeStruct((t, w), dt) for w, dt in zip(widths, out_dtypes)],
        grid=(t // ROW_TILE,),
        in_specs=[pl.BlockSpec((ROW_TILE, d), lambda i: (i, 0)),
                  pl.BlockSpec((1, d), lambda i: (0, 0)),
                  pl.BlockSpec((None, N_MOD, d), lambda i: (mrow(i), 0, 0))],
        out_specs=[pl.BlockSpec((ROW_TILE, w), lambda i: (i, 0)) for w in widths],
        compiler_params=_params(("parallel",)),
        name="norm_modulate",
    )(x, g.reshape(1, d), mod)
    return outs


def _final_norm_kernel(x_ref, g_ref, o_ref):
    x = x_ref[...]
    o_ref[...] = x * lax.rsqrt(jnp.mean(x * x, axis=-1, keepdims=True) + EPS) * g_ref[...]


def final_norm(x, g):
    t, d = x.shape
    return pl.pallas_call(
        _final_norm_kernel,
        out_shape=jax.ShapeDtypeStruct((t, d), f32),
        grid=(t // ROW_TILE,),
        in_specs=[pl.BlockSpec((ROW_TILE, d), lambda i: (i, 0)),
                  pl.BlockSpec((1, d), lambda i: (0, 0))],
        out_specs=pl.BlockSpec((ROW_TILE, d), lambda i: (i, 0)),
        compiler_params=_params(("parallel",)),
        name="final_norm",
    )(x, g.reshape(1, d))


def _mla_prep_kernel(p_ref, qg_ref, kvg_ref, wq_ref, wkv_ref, cos_ref, sin_ref,
                     q_ref, k_ref, v_ref):
    def rms(v, g):
        return v * lax.rsqrt(jnp.mean(v * v, axis=-1, keepdims=True) + EPS) * g

    cos, sin = cos_ref[...], sin_ref[...]

    def rope(t):
        return t * cos + pltpu.roll(t, MLA_ROPE, axis=1) * sin

    qn = rms(p_ref[:, :Q_LORA], qg_ref[...]).astype(bf16)
    kvn = rms(p_ref[:, Q_LORA:Q_LORA + KV_LORA], kvg_ref[...]).astype(bf16)
    scale = (MLA_NOPE + MLA_ROPE) ** -0.5 * math.log2(math.e)
    q = jnp.dot(qn, wq_ref[...], preferred_element_type=f32) * scale
    hw = 2 * LANES
    for h in range(MLA_HEADS):
        q_ref[:, h * hw:h * hw + LANES] = q[:, h * hw:h * hw + LANES].astype(bf16)
        q_ref[:, h * hw + LANES:(h + 1) * hw] = rope(q[:, h * hw + LANES:(h + 1) * hw]).astype(bf16)
    kv = jnp.dot(kvn, wkv_ref[...], preferred_element_type=f32)
    kpe = rope(p_ref[:, Q_LORA + KV_LORA:Q_LORA + KV_LORA + LANES]).astype(bf16)
    nk = MLA_HEADS * MLA_NOPE
    ones_col = jnp.where(lax.broadcasted_iota(jnp.int32, (kpe.shape[0], LANES), 1) == 0, 1.0, 0.0).astype(bf16)
    for h in range(MLA_HEADS):
        k_ref[:, h * hw:h * hw + LANES] = kv[:, h * MLA_NOPE:(h + 1) * MLA_NOPE].astype(bf16)
        k_ref[:, h * hw + LANES:(h + 1) * hw] = kpe
        v_ref[:, h * hw:h * hw + LANES] = kv[:, nk + h * MLA_V:nk + (h + 1) * MLA_V].astype(bf16)
        v_ref[:, h * hw + LANES:(h + 1) * hw] = ones_col


def mla_prep(proj, qn_g, kvn_g, wq, wkv, layer, cos_t, sin_t, tiles_per_batch):
    t = proj.shape[0]
    pw = proj.shape[1]
    qw = MLA_HEADS * 2 * LANES
    return pl.pallas_call(
        _mla_prep_kernel,
        out_shape=[jax.ShapeDtypeStruct((t, qw), bf16),
                   jax.ShapeDtypeStruct((t, qw), bf16),
                   jax.ShapeDtypeStruct((t, qw), bf16)],
        grid=(t // ROW_TILE,),
        in_specs=[pl.BlockSpec((ROW_TILE, pw), lambda i: (i, 0)),
                  pl.BlockSpec((1, Q_LORA), lambda i: (0, 0)),
                  pl.BlockSpec((1, KV_LORA), lambda i: (0, 0)),
                  pl.BlockSpec((None,) + wq.shape[1:], lambda i: (layer, 0, 0)),
                  pl.BlockSpec((None,) + wkv.shape[1:], lambda i: (layer, 0, 0)),
                  pl.BlockSpec((ROW_TILE, LANES), lambda i: (i % tiles_per_batch, 0)),
                  pl.BlockSpec((ROW_TILE, LANES), lambda i: (i % tiles_per_batch, 0))],
        out_specs=[pl.BlockSpec((ROW_TILE, qw), lambda i: (i, 0)),
                   pl.BlockSpec((ROW_TILE, qw), lambda i: (i, 0)),
                   pl.BlockSpec((ROW_TILE, qw), lambda i: (i, 0))],
        compiler_params=_params(("parallel",)),
        name="mla_prep",
    )(proj, qn_g.reshape(1, -1), kvn_g.reshape(1, -1), wq, wkv, cos_t, sin_t)


MLA_Q_TILE = 2048
MLA_SUB_TILE = 256


def _mla_attn_kernel(q_ref, k_ref, v_ref, o_ref):
    for r0 in range(0, q_ref.shape[0], MLA_SUB_TILE):
        rows = slice(r0, r0 + MLA_SUB_TILE)
        s = lax.dot_general(q_ref[rows, :], k_ref[...], (((1,), (1,)), ((), ())), preferred_element_type=f32)
        p = jnp.exp2((s - jnp.max(s, axis=-1, keepdims=True)).astype(bf16))
        ov = jnp.dot(p, v_ref[...], preferred_element_type=f32)
        o_ref[rows, :] = (ov[:, :MLA_V] / ov[:, MLA_V:MLA_V + 1]).astype(o_ref.dtype)


def mla_attention(q, k, v, n_batch, n_lat):
    p_len = q.shape[1]
    n_ctx = p_len - n_lat
    tq = MLA_Q_TILE
    while n_lat % tq:
        tq //= 2
    assert tq % MLA_SUB_TILE == 0
    assert n_lat % n_ctx == 0
    cb = n_lat // n_ctx
    o_lat = pl.pallas_call(
        _mla_attn_kernel,
        out_shape=jax.ShapeDtypeStruct((n_batch, n_lat, MLA_HEADS * MLA_V), bf16),
        grid=(n_batch, MLA_HEADS, n_lat // tq),
        in_specs=[pl.BlockSpec((None, tq, 2 * LANES), lambda b, h, i: (b, i, h)),
                  pl.BlockSpec((None, p_len, 2 * LANES), lambda b, h, i: (b, 0, h)),
                  pl.BlockSpec((None, p_len, 2 * LANES), lambda b, h, i: (b, 0, h))],
        out_specs=pl.BlockSpec((None, tq, MLA_V), lambda b, h, i: (b, i, h)),
        compiler_params=_params(("parallel", "parallel", "arbitrary")),
        name="mla_attention",
    )(q, k, v)
    o_ctx = pl.pallas_call(
        _mla_attn_kernel,
        out_shape=jax.ShapeDtypeStruct((n_batch, n_ctx, MLA_HEADS * MLA_V), bf16),
        grid=(n_batch, MLA_HEADS),
        in_specs=[pl.BlockSpec((None, n_ctx, 2 * LANES), lambda b, h: (b, cb, h)),
                  pl.BlockSpec((None, n_ctx, 2 * LANES), lambda b, h: (b, cb, h)),
                  pl.BlockSpec((None, n_ctx, 2 * LANES), lambda b, h: (b, cb, h))],
        out_specs=pl.BlockSpec((None, n_ctx, MLA_V), lambda b, h: (b, 0, h)),
        compiler_params=_params(("parallel", "parallel")),
        name="mla_attention_ctx",
    )(q, k, v)
    return jnp.concatenate([o_lat, o_ctx], axis=1)


NA_QROWS = 8
NA_BAND = 16


def _na_geometry(rows):
    assert rows % NA_QROWS == 0 and rows >= NA_BAND, rows
    nblk = rows // NA_QROWS
    starts = [min(max(NA_QROWS * i - WIN_R // 2, 0), rows - NA_BAND) for i in range(nblk)]
    if nblk <= 3:
        reps, pat = list(range(nblk)), list(range(nblk))
    else:
        reps = [0, 1, nblk - 1]
        pat = [0] + [1] * (nblk - 2) + [2]
        for i in range(1, nblk - 1):
            assert starts[i] == NA_QROWS * i - WIN_R // 2
    return nblk, starts, reps, pat


def na_bias_tables(rpb, rows):
    _, starts, reps, _ = _na_geometry(rows)
    nh = rpb.shape[0]
    wr = min(WIN_R, rows)
    n_off = 2 * WIN_R - 1
    qc = jnp.arange(GRID_W)
    cs = jnp.clip(qc - WIN_C // 2, 0, GRID_W - WIN_C)
    kc = jnp.arange(GRID_W)
    col_ok = (kc[None, :] >= cs[:, None]) & (kc[None, :] < cs[:, None] + WIN_C)
    col_off = jnp.clip(kc[None, :] - qc[:, None] + (WIN_C - 1), 0, 2 * WIN_C - 2)
    toe = jnp.where(col_ok[None, None], rpb.astype(f32)[:, :, col_off], NEG_BIG)
    toe = jnp.concatenate([toe, jnp.full((nh, 1, GRID_W, GRID_W), NEG_BIG, f32)], axis=1)
    sel = []
    for i in reps:
        qr = NA_QROWS * i + jnp.arange(NA_QROWS)
        kr = starts[i] + jnp.arange(NA_BAND)
        rs = jnp.clip(qr - wr // 2, 0, rows - wr)
        row_ok = (kr[None, :] >= rs[:, None]) & (kr[None, :] < rs[:, None] + wr)
        row_off = jnp.where(row_ok, kr[None, :] - qr[:, None] + (WIN_R - 1), n_off)
        sel.append(jax.nn.one_hot(row_off, n_off + 1, dtype=f32))
    sel = jnp.stack(sel, axis=0)
    tab = jnp.einsum("pakd,hdqc->phaqkc", sel, toe, precision=lax.Precision.HIGHEST)
    return tab.reshape(len(reps), nh, NA_QROWS * GRID_W, NA_BAND * GRID_W)


def _na_kernel(q_ref, k_ref, v_ref, bias_ref, o_ref, *, n_lat, starts, pat):
    scale = NA_DIM ** -0.5
    k_ctx = k_ref[n_lat:, :]
    v_ctx = v_ref[n_lat:, :]
    nt = (((1,), (1,)), ((), ()))
    qt = NA_QROWS * GRID_W
    bt = NA_BAND * GRID_W
    for i, (st, pt) in enumerate(zip(starts, pat)):
        q = q_ref[i * qt:(i + 1) * qt, :]
        kb = k_ref[st * GRID_W:st * GRID_W + bt, :]
        vb = v_ref[st * GRID_W:st * GRID_W + bt, :]
        s_loc = lax.dot_general(q, kb, nt, preferred_element_type=f32) * scale + bias_ref[pt]
        s_ctx = lax.dot_general(q, k_ctx, nt, preferred_element_type=f32) * scale
        m = jnp.maximum(jnp.max(s_loc, axis=-1, keepdims=True), jnp.max(s_ctx, axis=-1, keepdims=True))
        p_loc = jnp.exp(s_loc - m)
        p_ctx = jnp.exp(s_ctx - m)
        l = jnp.sum(p_loc, axis=-1, keepdims=True) + jnp.sum(p_ctx, axis=-1, keepdims=True)
        o = (jnp.dot(p_loc.astype(bf16), vb, preferred_element_type=f32)
             + jnp.dot(p_ctx.astype(bf16), v_ctx, preferred_element_type=f32))
        o_ref[i * qt:(i + 1) * qt, :] = (o / l).astype(o_ref.dtype)
    qz = q_ref[n_lat:, :]
    s = lax.dot_general(qz, k_ctx, nt, preferred_element_type=f32) * scale
    p = jnp.exp(s - jnp.max(s, axis=-1, keepdims=True))
    o = jnp.dot(p.astype(bf16), v_ctx, preferred_element_type=f32) / jnp.sum(p, axis=-1, keepdims=True)
    o_ref[n_lat:, :] = o.astype(o_ref.dtype)


def neighbourhood_attention(qkv, bias, n_batch, n_lat):
    p_len = qkv.shape[1]
    rows = n_lat // GRID_W
    _, starts, _, pat = _na_geometry(rows)
    npat = bias.shape[0]
    kern = functools.partial(_na_kernel, n_lat=n_lat, starts=tuple(starts), pat=tuple(pat))
    hspec = lambda off: pl.BlockSpec((None, p_len, NA_DIM), lambda h, b: (b, 0, off + h))
    return pl.pallas_call(
        kern,
        out_shape=jax.ShapeDtypeStruct((n_batch, p_len, NA_HEADS * NA_DIM), bf16),
        grid=(NA_HEADS, n_batch),
        in_specs=[hspec(0), hspec(NA_HEADS), hspec(2 * NA_HEADS),
                  pl.BlockSpec((npat, None) + bias.shape[2:], lambda h, b: (0, h, 0, 0))],
        out_specs=pl.BlockSpec((None, p_len, NA_DIM), lambda h, b: (b, 0, h)),
        compiler_params=_params(("parallel", "parallel")),
        name="neighbourhood_attention",
    )(qkv, qkv, qkv, bias)


def _segment_conv(x, w_ref, n_lat):
    p_len = x.shape[0]
    row = lax.broadcasted_iota(jnp.int32, (p_len, 1), 0)
    local = jnp.where(row < n_lat, row, row - n_lat)
    seg_len = jnp.where(row < n_lat, n_lat, p_len - n_lat)
    y = x * w_ref[CONV_PAD_L:CONV_PAD_L + 1, :]
    for j in range(CONV_W):
        off = j - CONV_PAD_L
        if off == 0:
            continue
        shifted = pltpu.roll(x, (-off) % p_len, axis=0)
        ok = (local + off >= 0) & (local + off < seg_len)
        y = y + jnp.where(ok, shifted, 0.0) * w_ref[j:j + 1, :]
    return y


SCAN_CHUNK = 256


SUBLANES = 8


def _chunk_scan(a, b, reverse, h_prev):
    n = a.shape[0]
    sub = lax.broadcasted_iota(jnp.int32, (n, 1), 0) % SUBLANES
    k = 1
    while k < SUBLANES:
        if reverse:
            a_s = pltpu.roll(a, n - k, axis=0)
            b_s = pltpu.roll(b, n - k, axis=0)
            ok = sub < SUBLANES - k
        else:
            a_s = pltpu.roll(a, k, axis=0)
            b_s = pltpu.roll(b, k, axis=0)
            ok = sub >= k
        b = b + a * jnp.where(ok, b_s, 0.0)
        a = a * jnp.where(ok, a_s, 1.0)
        k *= 2
    tiles = list(range(n // SUBLANES))
    out = [None] * len(tiles)
    for j in (tiles[::-1] if reverse else tiles):
        rows = slice(j * SUBLANES, (j + 1) * SUBLANES)
        h = b[rows] + a[rows] * h_prev
        h_prev = h[0:1] if reverse else h[SUBLANES - 1:SUBLANES]
        out[j] = h
    return jnp.concatenate(out, axis=0), h_prev


def _lru_kernel(lu_ref, ly_ref, cw_ref, cb_ref, wa_ref, ba_ref, wx_ref, bx_ref, lam_ref, o_ref,
                u_scr, a_scr, b_scr, h_scr, *, n_lat):
    p_len = lu_ref.shape[0]
    u_scr[...] = _segment_conv(lu_ref[...], cw_ref, n_lat) + cb_ref[...]
    segments = ((n_lat, p_len - n_lat), (0, n_lat))

    for d, reverse in enumerate((False, True)):
        ub = u_scr[...].astype(bf16)
        r = _sigmoid(jnp.dot(ub, wa_ref[d].astype(bf16), preferred_element_type=f32) + ba_ref[d:d + 1, :])
        g = _sigmoid(jnp.dot(ub, wx_ref[d].astype(bf16), preferred_element_type=f32) + bx_ref[d:d + 1, :])
        log_a = (-LRU_C * r) * _softplus(-lam_ref[d:d + 1, :])
        a_scr[...] = jnp.exp(log_a)
        th = jnp.tanh(log_a)
        b_scr[...] = jnp.sqrt(-2.0 * th / (1.0 - th)) * (g * u_scr[...])

        state = jnp.zeros((1, LANES), f32)
        for start, length in segments:
            n_chunks = length // SCAN_CHUNK

            def body(c, h_prev, start=start, n_chunks=n_chunks, reverse=reverse):
                ci = (n_chunks - 1 - c) if reverse else c
                rows = pl.ds(pl.multiple_of(start + ci * SCAN_CHUNK, SCAN_CHUNK), SCAN_CHUNK)
                h, h_last = _chunk_scan(a_scr[rows, :], b_scr[rows, :], reverse, h_prev)
                if d == 0:
                    h_scr[rows, :] = h
                else:
                    h_scr[rows, :] = h_scr[rows, :] + h
                return h_last

            state = lax.fori_loop(0, n_chunks, body, state)

    o_ref[...] = (h_scr[...] * _gelu_tanh(ly_ref[...])).astype(o_ref.dtype)


def rglru(luy, conv_w, conv_b, wa, ba, wx, bx, lam, n_batch, n_lat):
    p_len = luy.shape[1]
    nblk = LRU_W // LANES
    assert LRU_W // LRU_BLOCKS == LANES and n_lat % SCAN_CHUNK == 0 and (p_len - n_lat) % SCAN_CHUNK == 0
    vec = lambda rows: pl.BlockSpec((rows, LANES), lambda b, j: (0, j))
    return pl.pallas_call(
        functools.partial(_lru_kernel, n_lat=n_lat),
        out_shape=jax.ShapeDtypeStruct((n_batch, p_len, LRU_W), bf16),
        grid=(n_batch, nblk),
        in_specs=[pl.BlockSpec((None, p_len, LANES), lambda b, j: (b, 0, j)),
                  pl.BlockSpec((None, p_len, LANES), lambda b, j: (b, 0, nblk + j)),
                  vec(CONV_W), vec(1),
                  pl.BlockSpec((2, None, LANES, LANES), lambda b, j: (0, j, 0, 0)), vec(2),
                  pl.BlockSpec((2, None, LANES, LANES), lambda b, j: (0, j, 0, 0)), vec(2),
                  vec(2)],
        out_specs=pl.BlockSpec((None, p_len, LANES), lambda b, j: (b, 0, j)),
        scratch_shapes=[pltpu.VMEM((p_len, LANES), f32) for _ in range(4)],
        compiler_params=_params(("parallel", "parallel")),
        name="rglru",
    )(luy, luy, conv_w, conv_b.reshape(1, LRU_W), wa, ba, wx, bx, lam)


def _dn_pre_kernel(x_ref, w_ref, o_ref, *, n_lat):
    j = pl.program_id(1)
    u = _silu(_segment_conv(x_ref[...], w_ref, n_lat))
    nrm = u * lax.rsqrt(jnp.sum(u * u, axis=-1, keepdims=True) + EPS)
    qk_scale = jnp.where(j < DN_HEADS, DN_DK ** -0.5, 1.0)
    o_ref[...] = jnp.where(j < 2 * DN_HEADS, nrm * qk_scale, u).astype(o_ref.dtype)


def deltanet_pre(dqkv, conv_w, n_batch, n_lat, col_blocks):
    p_len = dqkv.shape[1]
    return pl.pallas_call(
        functools.partial(_dn_pre_kernel, n_lat=n_lat),
        out_shape=jax.ShapeDtypeStruct((n_batch, p_len, col_blocks * LANES), bf16),
        grid=(n_batch, col_blocks),
        in_specs=[pl.BlockSpec((None, p_len, LANES), lambda b, j: (b, 0, j)),
                  pl.BlockSpec((CONV_W, LANES), lambda b, j: (0, j))],
        out_specs=pl.BlockSpec((None, p_len, LANES), lambda b, j: (b, 0, j)),
        compiler_params=_params(("parallel", "parallel")),
        name="deltanet_pre",
    )(dqkv, conv_w)


def _dn_gate_kernel(ab_ref, alog_ref, dt_ref, o_ref):
    nh2 = 2 * DN_HEADS
    a_raw = ab_ref[:, :nh2]
    b_raw = ab_ref[:, nh2:2 * nh2]
    g = -jnp.exp(alog_ref[...]) * _softplus(a_raw + dt_ref[...])
    beta = _sigmoid(b_raw)
    n = g.shape[0]
    r = lax.broadcasted_iota(jnp.int32, (n, n), 0)
    c = lax.broadcasted_iota(jnp.int32, (n, n), 1)
    same = (r // DN_CHUNK) == (c // DN_CHUNK)
    hi = lax.Precision.HIGHEST
    pre = jnp.dot(jnp.where(same & (c <= r), 1.0, 0.0), g, preferred_element_type=f32, precision=hi)
    suf = jnp.dot(jnp.where(same & (c >= r), 1.0, 0.0), g, preferred_element_type=f32, precision=hi)
    tot = jnp.dot(jnp.where(same, 1.0, 0.0), g, preferred_element_type=f32, precision=hi)
    pad = jnp.zeros((n, LANES - 3 * nh2), f32)
    o_ref[...] = jnp.concatenate([pre[:, :DN_HEADS], suf[:, DN_HEADS:], beta, tot, pad], axis=-1)


def deltanet_gates(ab, col_block, a_log, dt_bias):
    t = ab.shape[0]
    nh2 = 2 * DN_HEADS
    return pl.pallas_call(
        _dn_gate_kernel,
        out_shape=jax.ShapeDtypeStruct((t, LANES), f32),
        grid=(t // ROW_TILE,),
        in_specs=[pl.BlockSpec((ROW_TILE, LANES), lambda i: (i, col_block)),
                  pl.BlockSpec((1, nh2), lambda i: (0, 0)),
                  pl.BlockSpec((1, nh2), lambda i: (0, 0))],
        out_specs=pl.BlockSpec((ROW_TILE, LANES), lambda i: (i, 0)),
        compiler_params=_params(("parallel",)),
        name="deltanet_gates",
    )(ab, a_log.reshape(1, nh2), dt_bias.reshape(1, nh2))


def _tri_inverse_minus_eye(low):
    c = low.shape[-1]
    bmm = functools.partial(jnp.einsum, "gij,gjk->gik", preferred_element_type=f32)
    x = -low
    acc = x
    k = 2
    while k < c:
        xb = x.astype(bf16)
        x = bmm(xb, xb)
        acc = acc + x + bmm(x.astype(bf16), acc.astype(bf16))
        k *= 2
    return acc


DN_GROUP_MAX = 17


def _dn_kernel(q_ref, k_ref, v_ref, z_ref, gcol_ref, grow_ref, ng_ref, o_ref,
               lhs_scr, inc_scr, gend_scr, acc_scr, *, n_lat, grp):
    p_len = q_ref.shape[0]
    cs = DN_CHUNK
    n_chunks = p_len // cs
    nc_lat = n_lat // cs
    nc_ctx = n_chunks - nc_lat
    nb = 2 * grp
    ri = lax.broadcasted_iota(jnp.int32, (nb, cs, cs), 1)
    ci = lax.broadcasted_iota(jnp.int32, (nb, cs, cs), 2)
    rev = lax.broadcasted_iota(jnp.int32, (nb, cs, cs), 0) >= grp
    ahead = jnp.where(rev, ri - ci, ci - ri)
    incl = ahead <= 0
    strict = ahead < 0

    def phase1(g, carry):
        c0 = g * grp
        rows = pl.ds(pl.multiple_of(c0 * cs, cs * grp), cs * grp)

        def both(x):
            x = x.astype(f32).reshape(grp, cs, x.shape[-1])
            return jnp.concatenate([x, x], axis=0)

        def per_dir(col):
            return jnp.concatenate([gcol_ref[rows, col + d:col + d + 1].reshape(grp, cs, 1) for d in range(2)], axis=0)

        q, k, v = both(q_ref[rows, :]), both(k_ref[rows, :]), both(v_ref[rows, :])
        gc, beta, gtot = per_dir(0), per_dir(2), per_dir(4)
        gr = jnp.concatenate([grow_ref[d, pl.ds(c0, grp), :] for d in range(2)], axis=0)[:, None, :]
        decay = jnp.where(incl, jnp.exp(jnp.where(incl, gc - gr, 0.0)), 0.0)
        e_gc = jnp.exp(gc)
        kb = k * beta
        gram = jnp.einsum("gik,gjk->gij", jnp.concatenate([kb, q], axis=1).astype(bf16), k.astype(bf16),
                          preferred_element_type=f32)
        t_m1 = _tri_inverse_minus_eye(jnp.where(strict, gram[:, :cs] * decay, 0.0))
        rhs = jnp.concatenate([v * beta, kb * e_gc], axis=2)
        w = rhs + jnp.einsum("gij,gjk->gik", t_m1.astype(bf16), rhs.astype(bf16), preferred_element_type=f32)
        k_dec = (k * jnp.exp(gtot - gc)).astype(bf16)
        a_in = (gram[:, cs:] * decay).astype(bf16)
        w_hi = w.astype(bf16)
        wv_lo = (w[:, :, :DN_DV] - w_hi[:, :, :DN_DV].astype(f32)).astype(bf16)
        kdt_w = jnp.einsum("gck,gcn->gkn", k_dec, w_hi, preferred_element_type=f32)
        s_inc = kdt_w[:, :, :DN_DV] + jnp.einsum("gck,gcn->gkn", k_dec, wv_lo, preferred_element_type=f32)
        ain_w = jnp.einsum("gij,gjn->gin", a_in, w_hi, preferred_element_type=f32)
        lhs = jnp.concatenate([-kdt_w[:, :, DN_DV:], q * e_gc - ain_w[:, :, DN_DV:]], axis=1).astype(bf16)
        g_end = jnp.broadcast_to(jnp.exp(gtot[:, 0:1, :]), (nb, 1, DN_DV))
        o_local = ain_w[:, :, :DN_DV]
        acc_scr[rows, :] = (o_local[:grp] + o_local[grp:]).reshape(grp * cs, DN_DV)
        for d in range(2):
            sl = slice(d * grp, (d + 1) * grp)
            dst = pl.ds(c0, grp)
            lhs_scr[d, dst] = lhs[sl]
            inc_scr[d, dst] = s_inc[sl]
            gend_scr[d, dst] = g_end[sl]
        return carry

    lax.fori_loop(0, n_chunks // grp, phase1, 0)

    def phase2(t, states):
        in_ctx = t < nc_ctx
        chunk = (jnp.where(in_ctx, nc_lat + t, t - nc_ctx),
                 jnp.where(in_ctx, n_chunks - 1 - t, nc_lat - 1 - (t - nc_ctx)))
        new_states = []
        for d in range(2):
            c = chunk[d]
            s = states[d]
            a = jnp.dot(lhs_scr[d, c], s.astype(bf16), preferred_element_type=f32)
            rows = pl.ds(pl.multiple_of(c * cs, cs), cs)
            acc_scr[rows, :] = acc_scr[rows, :] + a[DN_DK:]
            new_states.append(s * gend_scr[d, c] + inc_scr[d, c] + a[:DN_DK])
        return tuple(new_states)

    zero = jnp.zeros((DN_DK, DN_DV), f32)
    lax.fori_loop(0, n_chunks, phase2, (zero, zero))

    o = acc_scr[...]
    y = o * lax.rsqrt(jnp.mean(o * o, axis=-1, keepdims=True) + EPS) * ng_ref[...]
    o_ref[...] = (y * _silu(z_ref[...])).astype(o_ref.dtype)


def deltanet(qkv, dz_src, z_col0, gcol, grow, norm_g, n_batch, n_lat):
    p_len = qkv.shape[1]
    n_chunks = p_len // DN_CHUNK
    assert n_lat % DN_CHUNK == 0 and p_len % DN_CHUNK == 0
    grp = max(g for g in range(1, DN_GROUP_MAX + 1) if n_chunks % g == 0)
    hspec = lambda off: pl.BlockSpec((None, p_len, LANES), lambda b, h: (b, 0, off + h))
    return pl.pallas_call(
        functools.partial(_dn_kernel, n_lat=n_lat, grp=grp),
        out_shape=jax.ShapeDtypeStruct((n_batch, p_len, DN_HEADS * DN_DV), bf16),
        grid=(n_batch, DN_HEADS),
        in_specs=[hspec(0), hspec(DN_HEADS), hspec(2 * DN_HEADS),
                  pl.BlockSpec((None, p_len, LANES), lambda b, h: (b, 0, z_col0 + h)),
                  pl.BlockSpec((None, None, p_len, 8), lambda b, h: (b, h, 0, 0)),
                  pl.BlockSpec((None, None, 2, p_len // DN_CHUNK, DN_CHUNK), lambda b, h: (b, h, 0, 0, 0)),
                  pl.BlockSpec((1, DN_DV), lambda b, h: (0, 0))],
        out_specs=pl.BlockSpec((None, p_len, LANES), lambda b, h: (b, 0, h)),
        scratch_shapes=[pltpu.VMEM((2, n_chunks, DN_DK + DN_CHUNK, DN_DK), bf16),
                        pltpu.VMEM((2, n_chunks, DN_DK, DN_DV), f32),
                        pltpu.VMEM((2, n_chunks, 1, DN_DV), f32),
                        pltpu.VMEM((p_len, DN_DV), f32)],
        compiler_params=_params(("parallel", "parallel")),
        name="deltanet",
    )(qkv, qkv, qkv, dz_src, gcol, grow, norm_g.reshape(1, DN_DV))


def _merge_kernel(*refs):
    o_refs, h_ref, wg_refs, wb_ref, out_ref = (refs[:N_BRANCH], refs[N_BRANCH], refs[N_BRANCH + 1:2 * N_BRANCH + 1],
                                               refs[-2], refs[-1])
    h = h_ref[...]
    acc = None
    for i in range(N_BRANCH):
        y = jnp.dot(o_refs[i][...], wb_ref[i], preferred_element_type=f32)
        y = _sigmoid(jnp.dot(h, wg_refs[i][...], preferred_element_type=f32)) * y
        acc = y if acc is None else acc + y
    out_ref[...] = acc.astype(out_ref.dtype)


def merge_branches(branches, h, w_all, off_gate, w_branch, layer):
    t, bw = branches[0].shape
    d = w_branch.shape[3]
    tm, tn = ROW_TILE * 2, min(512, d)
    while t % tm:
        tm //= 2
    nj = d // tn
    assert off_gate % tn == 0 and d % tn == 0
    bspec = pl.BlockSpec((tm, bw), lambda j, i: (i, 0))
    gspecs = [pl.BlockSpec((None, d, tn),
                           functools.partial(lambda j, i, cb: (layer, 0, cb + j), cb=(off_gate + br * d) // tn))
              for br in range(N_BRANCH)]
    return pl.pallas_call(
        _merge_kernel,
        out_shape=jax.ShapeDtypeStruct((t, d), bf16),
        grid=(nj, t // tm),
        in_specs=([bspec] * N_BRANCH + [pl.BlockSpec((tm, d), lambda j, i: (i, 0))] + gspecs
                  + [pl.BlockSpec((None, N_BRANCH, bw, tn), lambda j, i: (layer, 0, 0, j))]),
        out_specs=pl.BlockSpec((tm, tn), lambda j, i: (i, j)),
        compiler_params=_params(("parallel", "parallel")),
        name="merge_branches",
    )(*branches, h, *([w_all] * N_BRANCH), w_branch)


def _out_proj_kernel(m_ref, w_ref, x_ref, mod_ref, o_ref, *, gate_row):
    y = jnp.dot(m_ref[...], w_ref[...], preferred_element_type=f32)
    o_ref[...] = x_ref[...] + mod_ref[gate_row:gate_row + 1, :] * y


def out_proj_residual(merged, w_out, layer, x, mod, gate_row, geom):
    t, d = x.shape
    n_batch, tpb, lat = geom
    mrow = functools.partial(_mod_row, tiles_per_batch=tpb, lat_tiles=lat, n_batch=n_batch)
    return pl.pallas_call(
        functools.partial(_out_proj_kernel, gate_row=gate_row),
        out_shape=jax.ShapeDtypeStruct((t, d), f32),
        grid=(t // ROW_TILE,),
        in_specs=[pl.BlockSpec((ROW_TILE, d), lambda i: (i, 0)),
                  pl.BlockSpec((None, d, d), lambda i: (layer, 0, 0)),
                  pl.BlockSpec((ROW_TILE, d), lambda i: (i, 0)),
                  pl.BlockSpec((None, N_MOD, d), lambda i: (mrow(i), 0, 0))],
        out_specs=pl.BlockSpec((ROW_TILE, d), lambda i: (i, 0)),
        compiler_params=_params(("parallel",)),
        name="out_proj_residual",
    )(merged, w_out, x, mod)


def _route_kernel(h_ref, rw_ref, rb_ref, sel_ref, cnt_ref, carry):
    i = pl.program_id(0)

    @pl.when(i == 0)
    def _():
        carry[...] = jnp.zeros_like(carry)

    h, w = h_ref[...], rw_ref[...]
    h_hi, w_hi = h.astype(bf16), w.astype(bf16)
    h_lo, w_lo = (h - h_hi.astype(f32)).astype(bf16), (w - w_hi.astype(f32)).astype(bf16)
    dot = functools.partial(jnp.dot, preferred_element_type=f32)
    logits = dot(h_hi, w_hi) + (dot(h_hi, w_lo) + dot(h_lo, w_hi)) + rb_ref[...]
    n = logits.shape[0]
    lane = lax.broadcasted_iota(jnp.int32, (n, LANES), 1).astype(f32)
    work = logits
    vals, idxs, hots = [], [], []
    for _ in range(TOP_K):
        m = jnp.max(work, axis=-1, keepdims=True)
        idx = jnp.min(jnp.where(work == m, lane, float(LANES)), axis=-1, keepdims=True)
        hot = lane == idx
        vals.append(m)
        idxs.append(idx)
        hots.append(hot)
        work = jnp.where(hot, NEG_BIG * 2.0, work)
    exps = [jnp.exp(v - vals[0]) for v in vals]
    den = exps[0]
    for e in exps[1:]:
        den = den + e
    onehot = jnp.zeros((n, LANES), f32)
    for hot in hots:
        onehot = onehot + jnp.where(hot, 1.0, 0.0)
    r = lax.broadcasted_iota(jnp.int32, (n, n), 0)
    c = lax.broadcasted_iota(jnp.int32, (n, n), 1)
    before = jnp.dot(jnp.where(c < r, 1.0, 0.0).astype(bf16), onehot.astype(bf16),
                     preferred_element_type=f32) + carry[...]
    out = jnp.zeros((n, LANES), f32)
    for kk in range(TOP_K):
        rank = jnp.sum(jnp.where(hots[kk], before, 0.0), axis=-1, keepdims=True)
        out = jnp.where(lane == float(kk), idxs[kk], out)
        out = jnp.where(lane == float(TOP_K + kk), rank, out)
        out = jnp.where(lane == float(2 * TOP_K + kk), exps[kk] / den, out)
    sel_ref[...] = out
    carry[...] = carry[...] + jnp.sum(onehot, axis=0, keepdims=True)
    cnt_ref[...] = carry[...]


def moe_route(h, router_w, router_b):
    t, d = h.shape
    e = router_w.shape[1]
    rw = jnp.zeros((d, LANES), f32).at[:, :e].set(router_w)
    rb = jnp.full((1, LANES), NEG_BIG, f32).at[0, :e].set(router_b)
    return pl.pallas_call(
        _route_kernel,
        out_shape=[jax.ShapeDtypeStruct((t, LANES), f32), jax.ShapeDtypeStruct((1, LANES), f32)],
        grid=(t // ROW_TILE,),
        in_specs=[pl.BlockSpec((ROW_TILE, d), lambda i: (i, 0)),
                  pl.BlockSpec((d, LANES), lambda i: (0, 0)),
                  pl.BlockSpec((1, LANES), lambda i: (0, 0))],
        out_specs=[pl.BlockSpec((ROW_TILE, LANES), lambda i: (i, 0)),
                   pl.BlockSpec((1, LANES), lambda i: (0, 0))],
        scratch_shapes=[pltpu.VMEM((1, LANES), f32)],
        compiler_params=_params(("arbitrary",)),
        name="moe_route",
    )(h, rw, rb)


def _dispatch_kernel(pos_hbm, x_ref, init_ref, xs_hbm, pos_smem, sem_pos, sem_rows):
    del init_ref
    i = pl.program_id(0)
    n_sel = ROW_TILE * TOP_K
    cp = pltpu.make_async_copy(pos_hbm.at[i], pos_smem, sem_pos)
    cp.start()
    cp.wait()

    def issue(r, c):
        src = x_ref.at[pl.ds(r, 1)]
        for kk in range(TOP_K):
            pltpu.make_async_copy(src, xs_hbm.at[pl.ds(pos_smem[r * TOP_K + kk], 1)], sem_rows).start(priority=kk % 2)
        return c

    lax.fori_loop(0, ROW_TILE, issue, 0, unroll=4)
    pltpu.make_async_copy(xs_hbm.at[pl.ds(0, n_sel)], xs_hbm.at[pl.ds(0, n_sel)], sem_rows).wait()


def moe_dispatch(h, pos, n_slots):
    t, d = h.shape
    tiles = t // ROW_TILE
    init = jnp.zeros((n_slots, d), h.dtype)
    return pl.pallas_call(
        _dispatch_kernel,
        out_shape=jax.ShapeDtypeStruct((n_slots, d), h.dtype),
        grid=(tiles,),
        in_specs=[pl.BlockSpec(memory_space=pl.ANY),
                  pl.BlockSpec((ROW_TILE, d), lambda i: (i, 0)),
                  pl.BlockSpec(memory_space=pl.ANY)],
        out_specs=pl.BlockSpec(memory_space=pl.ANY),
        scratch_shapes=[pltpu.SMEM((ROW_TILE * TOP_K,), jnp.int32),
                        pltpu.SemaphoreType.DMA, pltpu.SemaphoreType.DMA],
        input_output_aliases={2: 0},
        compiler_params=_params(("arbitrary",)),
        name="moe_dispatch",
    )(pos.reshape(tiles, ROW_TILE * TOP_K), h, init)


def _expert_kernel(te_ref, xs_ref, w1_ref, b1_ref, sel_ref, w2_ref, b2_ref, ys_ref):
    del te_ref
    x = jnp.concatenate(_unpack_halves(xs_ref[...]), axis=1).astype(bf16)
    u = jnp.dot(x, w1_ref[...], preferred_element_type=f32) + b1_ref[...]
    n = u.shape[1]
    lin = pltpu.roll(u, n - 1, axis=1)
    glu = jnp.minimum(u, SWIGLU_LIMIT)
    lin = jnp.clip(lin, -SWIGLU_LIMIT, SWIGLU_LIMIT)
    act = glu * _sigmoid(SWIGLU_ALPHA * glu) * (lin + 1.0)
    even = lax.broadcasted_iota(jnp.int32, u.shape, 1) % 2 == 0
    act = jnp.where(even, act, 0.0).astype(bf16)
    act = jnp.dot(act, sel_ref[...], preferred_element_type=f32).astype(bf16)
    ys_ref[...] = _pack_halves(jnp.dot(act, w2_ref[...], preferred_element_type=f32) + b2_ref[...])


EXPERT_TILE = 512


def moe_experts(xs, tile_expert, layer, w1, b1, w2, b2):
    s, dh = xs.shape
    d = 2 * dh
    ff2 = w1.shape[3]
    ff = ff2 // 2
    sel = (jnp.arange(ff2)[:, None] == 2 * jnp.arange(ff)[None, :]).astype(bf16)
    grid_spec = pltpu.PrefetchScalarGridSpec(
        num_scalar_prefetch=1,
        grid=(s // EXPERT_TILE,),
        in_specs=[pl.BlockSpec((EXPERT_TILE, dh), lambda i, te: (i, 0)),
                  pl.BlockSpec((None, None, d, ff2), lambda i, te: (layer, te[i], 0, 0)),
                  pl.BlockSpec((None, None, 1, ff2), lambda i, te: (layer, te[i], 0, 0)),
                  pl.BlockSpec((ff2, ff), lambda i, te: (0, 0)),
                  pl.BlockSpec((None, None, ff, d), lambda i, te: (layer, te[i], 0, 0)),
                  pl.BlockSpec((None, None, 1, d), lambda i, te: (layer, te[i], 0, 0))],
        out_specs=pl.BlockSpec((EXPERT_TILE, dh), lambda i, te: (i, 0)),
    )
    return pl.pallas_call(
        _expert_kernel,
        out_shape=jax.ShapeDtypeStruct((s, dh), jnp.uint32),
        grid_spec=grid_spec,
        compiler_params=_params(("arbitrary",)),
        name="moe_experts",
    )(tile_expert, xs, w1, b1, sel, w2, b2)


def _combine_kernel(pos_hbm, ys_hbm, x_ref, wt_ref, mod_ref, o_ref, buf, pos_smem, sem_pos, sem_rows, *, gate_row):
    i = pl.program_id(0)
    n_sel = ROW_TILE * TOP_K
    cp = pltpu.make_async_copy(pos_hbm.at[i], pos_smem, sem_pos)
    cp.start()
    cp.wait()

    def issue(r, c):
        for kk in range(TOP_K):
            pltpu.make_async_copy(ys_hbm.at[pl.ds(pos_smem[r * TOP_K + kk], 1)],
                                  buf.at[kk, pl.ds(r, 1)], sem_rows).start(priority=kk % 2)
        return c

    lax.fori_loop(0, ROW_TILE, issue, 0, unroll=4)
    pltpu.make_async_copy(ys_hbm.at[pl.ds(0, n_sel)], ys_hbm.at[pl.ds(0, n_sel)], sem_rows).wait()
    m = buf.shape[2]
    acc_lo = acc_hi = None
    for kk in range(TOP_K):
        wk = wt_ref[:, 2 * TOP_K + kk:2 * TOP_K + kk + 1]
        lo, hi = _unpack_halves(buf[kk])
        acc_lo = wk * lo if acc_lo is None else acc_lo + wk * lo
        acc_hi = wk * hi if acc_hi is None else acc_hi + wk * hi
    o_ref[:, :m] = x_ref[:, :m] + mod_ref[gate_row:gate_row + 1, :m] * acc_lo
    o_ref[:, m:] = x_ref[:, m:] + mod_ref[gate_row:gate_row + 1, m:] * acc_hi


def moe_combine(ys, pos, sel, x, mod, gate_row, geom):
    t, d = x.shape
    tiles = t // ROW_TILE
    n_batch, tpb, lat = geom
    mrow = functools.partial(_mod_row, tiles_per_batch=tpb, lat_tiles=lat, n_batch=n_batch)
    return pl.pallas_call(
        functools.partial(_combine_kernel, gate_row=gate_row),
        out_shape=jax.ShapeDtypeStruct((t, d), f32),
        grid=(tiles,),
        in_specs=[pl.BlockSpec(memory_space=pl.ANY),
                  pl.BlockSpec(memory_space=pl.ANY),
                  pl.BlockSpec((ROW_TILE, d), lambda i: (i, 0)),
                  pl.BlockSpec((ROW_TILE, LANES), lambda i: (i, 0)),
                  pl.BlockSpec((None, N_MOD, d), lambda i: (mrow(i), 0, 0))],
        out_specs=pl.BlockSpec((ROW_TILE, d), lambda i: (i, 0)),
        scratch_shapes=[pltpu.VMEM((TOP_K, ROW_TILE, d // 2), jnp.uint32),
                        pltpu.SMEM((ROW_TILE * TOP_K,), jnp.int32),
                        pltpu.SemaphoreType.DMA, pltpu.SemaphoreType.DMA],
        compiler_params=_params(("arbitrary",)),
        name="moe_combine",
    )(pos.reshape(tiles, ROW_TILE * TOP_K), ys, x, sel, mod)


def moe_ffn_residual(h, h_packed, x, mod, gate_row, geom, router_w, router_b, layer, w1, b1, w2, b2):
    t, d = h.shape
    n_exp = router_w.shape[1]
    sel, counts = moe_route(h, router_w, router_b)
    eidx = sel[:, :TOP_K].astype(jnp.int32)
    rank = sel[:, TOP_K:2 * TOP_K].astype(jnp.int32)
    counts = counts[0, :n_exp].astype(jnp.int32)
    padded = ((counts + EXPERT_TILE - 1) // EXPERT_TILE) * EXPERT_TILE
    ends = jnp.cumsum(padded)
    starts = ends - padded
    pos = starts[eidx] + rank
    assert (t * TOP_K) % EXPERT_TILE == 0
    n_tiles = (t * TOP_K) // EXPERT_TILE + n_exp
    tile_start = jnp.arange(n_tiles, dtype=jnp.int32) * EXPERT_TILE
    tile_expert = jnp.minimum(jnp.sum((ends[None, :] <= tile_start[:, None]).astype(jnp.int32), axis=1), n_exp - 1)
    xs = moe_dispatch(h_packed, pos, n_tiles * EXPERT_TILE)
    ys = moe_experts(xs, tile_expert, layer, w1, b1, w2, b2)
    return moe_combine(ys, pos, sel, x, mod, gate_row, geom)


def _rope_perm(w):
    lead = w.shape[:-1]
    nf = MLA_ROPE // 4
    wr = w.reshape(*lead, 2, 2, nf)
    return jnp.stack([-wr[..., 1, :], wr[..., 0, :]], axis=-2).reshape(*lead, MLA_ROPE)


NA_COLS = 3 * NA_HEADS * NA_DIM
LRU_COLS = 2 * LRU_W
DN_COLS = DN_HEADS * (2 * DN_DK + DN_DV) + DN_HEADS * DN_DV
MLA_COLS = Q_LORA + KV_LORA + 2 * LANES


def regroup_w_in(w_in, d_model):
    sizes = (Q_LORA, KV_LORA, MLA_ROPE, NA_HEADS * NA_DIM, NA_HEADS * NA_DIM, NA_HEADS * NA_DIM, LRU_W, LRU_W,
             DN_HEADS * (2 * DN_DK + DN_DV), DN_HEADS * DN_DV, 2 * DN_HEADS, 2 * DN_HEADS, N_BRANCH * d_model)
    parts, s = [], 0
    for n in sizes:
        parts.append(w_in[:, :, s:s + n])
        s += n
    assert s == w_in.shape[2]
    (w_qc, w_kvc, w_kpe, w_naq, w_nak, w_nav, w_lu, w_ly, w_dqkv, w_dz, w_da, w_db, w_gt) = parts
    pad = jnp.zeros(w_in.shape[:2] + (LANES - 4 * DN_HEADS,), w_in.dtype)
    w = jnp.concatenate([w_naq, w_nak, w_nav, w_lu, w_ly, w_dqkv, w_dz, w_gt,
                         w_qc, w_kvc, w_kpe, _rope_perm(w_kpe), w_da, w_db, pad], axis=2).astype(bf16)
    off_na = 0
    off_lru = off_na + NA_COLS
    off_dn = off_lru + LRU_COLS
    off_gate = off_dn + DN_COLS
    off_mla = off_gate + N_BRANCH * d_model
    assert w.shape[2] == off_mla + MLA_COLS
    return w, (off_na, off_lru, off_dn, off_gate, off_mla)


def rope_tables(n_lat, p_len):
    t = jnp.arange(n_lat)
    row = (t // GRID_W).astype(f32)
    col = (t % GRID_W).astype(f32)
    nf = MLA_ROPE // 4
    inv_freq = ROPE_BASE ** (-jnp.arange(nf, dtype=f32) / nf)
    ang = jnp.concatenate([row[:, None] * inv_freq] * 2 + [col[:, None] * inv_freq] * 2, axis=1)
    zeros = jnp.zeros((n_lat, LANES - MLA_ROPE), f32)
    cos_t = jnp.concatenate([jnp.cos(ang), zeros], axis=1)
    sin_t = jnp.concatenate([jnp.sin(ang), zeros], axis=1)
    n_ctx = p_len - n_lat
    ctx_cos = jnp.concatenate([jnp.ones((n_ctx, MLA_ROPE), f32), jnp.zeros((n_ctx, LANES - MLA_ROPE), f32)], axis=1)
    return (jnp.concatenate([cos_t, ctx_cos], axis=0),
            jnp.concatenate([sin_t, jnp.zeros((n_ctx, LANES), f32)], axis=0))


def prepare_weights(w_in, mla_wq_up, mla_wkv_up, w_branch, w_out, exp_w1, exp_b1, exp_w2, exp_b2, d_model):
    depth = w_in.shape[0]
    w_all, offs = regroup_w_in(w_in, d_model)
    wq = mla_wq_up.reshape(depth, Q_LORA, MLA_HEADS, MLA_NOPE + MLA_ROPE)
    wq = jnp.concatenate([wq, _rope_perm(wq[..., MLA_NOPE:])], axis=-1).reshape(depth, Q_LORA, -1).astype(bf16)
    wkv = mla_wkv_up.reshape(depth, KV_LORA, MLA_HEADS, MLA_NOPE + MLA_V)
    wkv = jnp.concatenate([wkv[..., :MLA_NOPE].reshape(depth, KV_LORA, -1),
                           wkv[..., MLA_NOPE:].reshape(depth, KV_LORA, -1)], axis=2).astype(bf16)
    return dict(w_all=w_all, offs=offs, wq=wq, wkv=wkv, w_branch=w_branch.astype(bf16), w_out=w_out.astype(bf16),
                exp_w1=exp_w1.astype(bf16), exp_b1=exp_b1[:, :, None, :], exp_w2=exp_w2.astype(bf16),
                exp_b2=exp_b2[:, :, None, :])


def token_mixer_branches(h, l, wts, geom, n_lat, cos_t, sin_t, na_bias, mla_qn_g, mla_kvn_g, lru_conv_w, lru_conv_b,
                         lru_wa, lru_ba, lru_wx, lru_bx, lru_lam, dn_conv_w, dn_a_log, dn_dt_bias, dn_norm_g):
    n_batch, tpb, lat_tiles = geom
    t, d = h.shape
    p_len = t // n_batch
    w_all = wts["w_all"]
    off_na, off_lru, off_dn, off_gate, off_mla = wts["offs"]

    p_mla = matmul(h, w_all, l, off_mla, MLA_COLS, f32, 1024, 256, "proj_mla")
    q, k, v = mla_prep(p_mla, mla_qn_g, mla_kvn_g, wts["wq"], wts["wkv"], l, cos_t, sin_t, tpb)
    o_a = mla_attention(q.reshape(n_batch, p_len, -1), k.reshape(n_batch, p_len, -1),
                        v.reshape(n_batch, p_len, -1), n_batch, n_lat)

    p_na = matmul(h, w_all, l, off_na, NA_COLS, bf16, 1024, 512, "proj_na")
    o_b = neighbourhood_attention(p_na.reshape(n_batch, p_len, -1), na_bias, n_batch, n_lat)

    p_lru = matmul(h, w_all, l, off_lru, LRU_COLS, f32, 1024, 512, "proj_lru")
    o_c = rglru(p_lru.reshape(n_batch, p_len, -1), lru_conv_w, lru_conv_b, lru_wa, lru_ba, lru_wx, lru_bx, lru_lam,
                n_batch, n_lat)

    p_dn = matmul(h, w_all, l, off_dn, DN_COLS, f32, 1024, 512, "proj_dn").reshape(n_batch, p_len, -1)
    qkv_dn = deltanet_pre(p_dn, dn_conv_w, n_batch, n_lat, 3 * DN_HEADS)
    gates = deltanet_gates(p_mla, (MLA_COLS - LANES) // LANES, dn_a_log, dn_dt_bias).reshape(n_batch, p_len, LANES)
    nh = DN_HEADS

    def per_head(cols):
        return jnp.transpose(cols.reshape(n_batch, p_len, 2, nh), (0, 3, 1, 2))

    gc_cols = per_head(gates[..., :2 * nh])
    gcol = jnp.concatenate([gc_cols, per_head(gates[..., 2 * nh:4 * nh]), per_head(gates[..., 4 * nh:6 * nh]),
                            jnp.zeros((n_batch, nh, p_len, 2), f32)], axis=-1)
    grow = jnp.transpose(gc_cols, (0, 1, 3, 2)).reshape(n_batch, nh, 2, p_len // DN_CHUNK, DN_CHUNK)
    o_d = deltanet(qkv_dn, p_dn, 3 * DN_HEADS, gcol, grow, dn_norm_g, n_batch, n_lat)

    return o_a, o_b, o_c, o_d


def kernel(x, c, ctx, c_ctx, ada_w, ada_b, norm_mix_g, norm_ffn_g, w_in, mla_qn_g, mla_wq_up, mla_kvn_g, mla_wkv_up, na_rpb, lru_conv_w, lru_conv_b, lru_wa, lru_ba, lru_wx, lru_bx, lru_lam, dn_conv_w, dn_a_log, dn_dt_bias, dn_norm_g, w_branch, w_out, router_w, router_b, exp_w1, exp_b1, exp_w2, exp_b2, final_g):
    n_batch, n_lat, d = x.shape
    n_ctx = ctx.shape[1]
    depth = ada_w.shape[0]
    p_len = n_lat + n_ctx
    t = n_batch * p_len
    assert n_lat % ROW_TILE == 0 and n_ctx % ROW_TILE == 0 and n_lat % GRID_W == 0
    tpb, lat_tiles = p_len // ROW_TILE, n_lat // ROW_TILE
    geom = (n_batch, tpb, lat_tiles)

    stream = jnp.concatenate([x, ctx], axis=1).reshape(t, d)
    pad_rows = (-(n_batch + 1)) % 8
    cond = jnp.concatenate([c, c_ctx[None, :], jnp.zeros((pad_rows, d), f32)], axis=0)
    mods = ada_modulation(cond, ada_w, ada_b)[:, :n_batch + 1].reshape(depth, n_batch + 1, N_MOD, d)
    cos_t, sin_t = rope_tables(n_lat, p_len)

    wts = prepare_weights(w_in, mla_wq_up, mla_wkv_up, w_branch, w_out, exp_w1, exp_b1, exp_w2, exp_b2, d)
    bias_tabs = [na_bias_tables(na_rpb[l], n_lat // GRID_W) for l in range(depth)]

    for l in range(depth):
        mod = mods[l]
        (h,) = norm_modulate(stream, norm_mix_g[l], mod, 0, geom, (bf16,))
        o_a, o_b, o_c, o_d = token_mixer_branches(
            h, l, wts, geom, n_lat, cos_t, sin_t, bias_tabs[l], mla_qn_g[l], mla_kvn_g[l], lru_conv_w[l],
            lru_conv_b[l], lru_wa[l], lru_ba[l], lru_wx[l], lru_bx[l], lru_lam[l], dn_conv_w[l], dn_a_log[l],
            dn_dt_bias[l], dn_norm_g[l])

        merged = merge_branches((o_a.reshape(t, -1), o_b.reshape(t, -1), o_c.reshape(t, -1), o_d.reshape(t, -1)),
                                h, wts["w_all"], wts["offs"][3], wts["w_branch"], l)
        stream = out_proj_residual(merged, wts["w_out"], l, stream, mod, 2, geom)

        h2, h2_packed = norm_modulate(stream, norm_ffn_g[l], mod, 3, geom, (f32, jnp.uint32))
        stream = moe_ffn_residual(h2, h2_packed, stream, mod, 5, geom, router_w[l], router_b[l], l,
                                  wts["exp_w1"], wts["exp_b1"], wts["exp_w2"], wts["exp_b2"])

    out = final_norm(stream, final_g)
    return out.reshape(n_batch, p_len, d)[:, :n_lat]
```
